```python
import math
import jax
import jax.numpy as jnp
from jax import lax
import numpy as np

D_MODEL = 1024
BATCH = 4
SEQ = 8192
DEPTH = 4
DEC_BATCH = 8
DEC_SEQ = 32
PAST_LEN = 1024

CHUNK = 64
N_MIXERS = 3
HEAD_DIM = 64
N_HEADS = D_MODEL // HEAD_DIM
N_KV_HEADS = 4
GROUP = N_HEADS // N_KV_HEADS
WINDOW = 128
BAND_CHUNKS = WINDOW // CHUNK
NUM_BUCKETS = 32
MAX_DISTANCE = 128
RWKV_HEAD = 64
RWKV_HEADS = D_MODEL // RWKV_HEAD
D_DECAY_LORA = 64
D_AAA_LORA = 64
D_GATE_LORA = 128
GN_EPS = RWKV_HEAD * 1e-5
CONV_WIDTH = 3
D_FF = 4 * D_MODEL
RMS_EPS = 1e-6

N_A = len(range(0, DEPTH, N_MIXERS))
N_B = len(range(1, DEPTH, N_MIXERS))
N_C = len(range(2, DEPTH, N_MIXERS))

kernel_name = "hybrid_swa_rwkv7_shortconv_stream_step"

F32 = jnp.float32


def _rmsnorm(x, g):
    xf = x.astype(F32)
    y = xf * lax.rsqrt(jnp.mean(xf * xf, axis=-1, keepdims=True) + RMS_EPS)
    return (y * g.astype(F32)).astype(x.dtype)


def _rel_bucket(rp):
    nb = NUM_BUCKETS // 2
    ret = (rp > 0).astype(jnp.int32) * nb
    n = jnp.abs(rp)
    max_exact = nb // 2
    nf = jnp.maximum(n, 1).astype(F32)
    large = max_exact + (jnp.log(nf / max_exact) / math.log(MAX_DISTANCE / max_exact)
                         * (nb - max_exact)).astype(jnp.int32)
    large = jnp.minimum(large, nb - 1)
    return ret + jnp.where(n < max_exact, n, large)


def _rel_bias(table, q_len, k_len, k_offset):
    i = jnp.arange(q_len)[:, None]
    j = jnp.arange(k_len)[None, :]
    b = table[_rel_bucket(j - k_offset - i)]
    return jnp.transpose(b, (2, 0, 1)).reshape(N_KV_HEADS, GROUP, q_len, k_len)


def _attn_core(q, k, v, bias, mask, sinks):
    s = jnp.einsum('bnqhgd,bnkhd->bnhgqk', q, k).astype(F32) * (HEAD_DIM ** -0.5) + bias.astype(F32)
    if mask is not None:
        s = jnp.where(mask[None, :, None, None], s, -jnp.inf)
    sink = sinks.astype(F32).reshape(1, 1, N_KV_HEADS, GROUP, 1, 1)
    m = jnp.maximum(jnp.max(s, axis=-1, keepdims=True), sink)
    p = jnp.exp(s - m)
    den = jnp.sum(p, axis=-1, keepdims=True) + jnp.exp(sink - m)
    return jnp.einsum('bnhgqk,bnkhd->bnqhgd', (p / den).astype(v.dtype), v)


def _qkv(h, w):
    B, T, _ = h.shape
    qkv = h @ w
    nq, nk = N_HEADS * HEAD_DIM, N_KV_HEADS * HEAD_DIM
    q = qkv[..., :nq].reshape(B, T, N_HEADS, HEAD_DIM)
    k = qkv[..., nq:nq + nk].reshape(B, T, N_KV_HEADS, HEAD_DIM)
    v = qkv[..., nq + nk:].reshape(B, T, N_KV_HEADS, HEAD_DIM)
    return q, k, v


def _attn_prompt(h, P, j):
    B, T, _ = h.shape
    nc = T // CHUNK
    q, k, v = _qkv(h, P['a_w_qkv'][j])
    q = q.reshape(B, nc, CHUNK, N_KV_HEADS, GROUP, HEAD_DIM)

    def band(t):
        tc = t.reshape(B, nc, CHUNK, N_KV_HEADS, HEAD_DIM)
        tp = jnp.pad(tc, ((0, 0), (BAND_CHUNKS, 0), (0, 0), (0, 0), (0, 0)))
        return jnp.concatenate([tp[:, s:s + nc] for s in range(BAND_CHUNKS + 1)], axis=2)

    kl = (BAND_CHUNKS + 1) * CHUNK
    key_chunk = jnp.arange(nc)[:, None, None] - BAND_CHUNKS + (jnp.arange(kl) // CHUNK)[None, None, :]
    mask = key_chunk >= 0
    bias = _rel_bias(P['rel_bias_table'], CHUNK, kl, BAND_CHUNKS * CHUNK)
    o = _attn_core(q, band(k), band(v), bias, mask, P['a_sinks'][j])
    y = o.reshape(B, T, N_HEADS * HEAD_DIM) @ P['a_w_o'][j]
    return y, k[:, -WINDOW:], v[:, -WINDOW:]


def _attn_sample(h, ck, cv, P, j):
    B, S, _ = h.shape
    W = ck.shape[1]
    q, k, v = _qkv(h, P['a_w_qkv'][j])
    q = q.reshape(B, 1, S, N_KV_HEADS, GROUP, HEAD_DIM)
    kf = jnp.concatenate([ck.astype(k.dtype), k], axis=1)
    vf = jnp.concatenate([cv.astype(v.dtype), v], axis=1)
    bias = _rel_bias(P['rel_bias_table'], S, W + S, W)
    o = _attn_core(q, kf[:, None], vf[:, None], bias, None, P['a_sinks'][j])
    y = o.reshape(B, S, N_HEADS * HEAD_DIM) @ P['a_w_o'][j]
    return y, kf[:, -W:], vf[:, -W:]


def _rwkv7(x, shift0, S0, P, j):
    B, T, D = x.shape
    mu = P['b_mu'][j]
    xx = jnp.concatenate([shift0[:, None].astype(x.dtype), x[:, :-1]], axis=1) - x
    xr, xw, xk, xv, xa, xg = [x + xx * mu[i] for i in range(6)]
    w_rkv = P['b_w_rkv'][j]
    r = (xr @ w_rkv[0]).astype(F32)
    k = (xk @ w_rkv[1]).astype(F32)
    v = (xv @ w_rkv[2]).astype(F32)
    w = -jax.nn.softplus(-(P['b_w0'][j] + jnp.tanh(xw @ P['b_w1'][j]) @ P['b_w2'][j]).astype(F32)) - 0.5
    decay = jnp.exp(-jnp.exp(w))
    a = jax.nn.sigmoid((P['b_a0'][j] + (xa @ P['b_a1'][j]) @ P['b_a2'][j]).astype(F32))
    g = jax.nn.sigmoid(xg @ P['b_g1'][j]) @ P['b_g2'][j]
    heads = lambda t: t.reshape(B, T, RWKV_HEADS, RWKV_HEAD)
    kk = heads(k * P['b_k_k'][j].astype(F32))
    kk = kk * lax.rsqrt(jnp.maximum(jnp.sum(kk * kk, axis=-1, keepdims=True), 1e-24))
    k = k * (1.0 + (a - 1.0) * P['b_k_a'][j].astype(F32))
    r_h, k_h, v_h, a_h, d_h = heads(r), heads(k), heads(v), heads(a), heads(decay)

    def step(S, inp):
        r_t, k_t, v_t, kk_t, a_t, d_t = inp
        sa = jnp.einsum('bhvk,bhk->bhv', S, -kk_t)
        S = S * d_t[:, :, None, :] + sa[..., None] * (kk_t * a_t)[:, :, None, :] + v_t[..., None] * k_t[:, :, None, :]
        return S, jnp.einsum('bhvk,bhk->bhv', S, r_t)

    xs = tuple(jnp.moveaxis(t, 1, 0) for t in (r_h, k_h, v_h, kk, a_h, d_h))
    S_fin, y = lax.scan(step, S0.astype(F32), xs)
    y = jnp.moveaxis(y, 0, 1)
    mean = jnp.mean(y, axis=-1, keepdims=True)
    var = jnp.mean(jnp.square(y - mean), axis=-1, keepdims=True)
    yn = ((y - mean) * lax.rsqrt(var + GN_EPS)).reshape(B, T, D) * P['b_ln_w'][j].astype(F32) + P['b_ln_b'][j].astype(F32)
    bonus = (jnp.sum(r_h * k_h * P['b_r_k'][j].astype(F32), axis=-1, keepdims=True) * v_h).reshape(B, T, D)
    out = ((yn + bonus).astype(x.dtype) * g) @ P['b_w_o'][j]
    return out, S_fin, x[:, -1]


def _short_conv(x, prev_u, P, j):
    T = x.shape[1]
    bg, cg, hh = jnp.split(x @ P['c_w_in'][j], 3, axis=-1)
    u = cg * hh
    up = jnp.concatenate([prev_u.astype(u.dtype), u], axis=1)
    cw = P['c_conv_w'][j]
    y = sum(up[:, t:t + T] * cw[t] for t in range(CONV_WIDTH))
    return (bg * y) @ P['c_w_out'][j], up[:, -(CONV_WIDTH - 1):]


def _sqrelu_mlp(x, w1, w2):
    h = jax.nn.relu(x @ w1)
    return (h * h) @ w2


def _trunk(x, prompt, a_k, a_v, b_wkv, b_shift, c_conv, P):
    B = x.shape[0]
    nk, nv, nwkv, nsh, ncv = [], [], [], [], []
    g = P['norm_g']
    for i in range(DEPTH):
        kind, j = i % N_MIXERS, i // N_MIXERS
        h = _rmsnorm(x, g[i, 0])
        if kind == 0:
            if prompt:
                m, k_new, v_new = _attn_prompt(h, P, j)
            else:
                m, k_new, v_new = _attn_sample(h, a_k[j], a_v[j], P, j)
            nk.append(k_new)
            nv.append(v_new)
        elif kind == 1:
            if prompt:
                sh0 = jnp.zeros((B, D_MODEL), x.dtype)
                s0 = jnp.zeros((B, RWKV_HEADS, RWKV_HEAD, RWKV_HEAD), F32)
            else:
                sh0, s0 = b_shift[j], b_wkv[j]
            m, s_new, sh_new = _rwkv7(h, sh0, s0, P, j)
            nwkv.append(s_new)
            nsh.append(sh_new)
        else:
            u0 = jnp.zeros((B, CONV_WIDTH - 1, D_MODEL), x.dtype) if prompt else c_conv[j]
            m, u_new = _short_conv(h, u0, P, j)
            ncv.append(u_new)
        x = x + _rmsnorm(m, g[i, 1])
        x = x + _rmsnorm(_sqrelu_mlp(_rmsnorm(x, g[i, 2]), P['mlp_w1'][i], P['mlp_w2'][i]), g[i, 3])
    return x, jnp.stack(nk), jnp.stack(nv), jnp.stack(nwkv), jnp.stack(nsh), jnp.stack(ncv)


def setup_inputs(seed: int = 0) -> dict:
    key = jax.random.key(seed)
    ks = iter(jax.random.split(key, 48))
    nrm = lambda shape, scale: scale * jax.random.normal(next(ks), shape, F32)
    D = D_MODEL
    win_rows = min(WINDOW, PAST_LEN)
    d_qkv = (N_HEADS + 2 * N_KV_HEADS) * HEAD_DIM
    return {
        'x_prompt': nrm((BATCH, SEQ, D), 1.0),
        'x_sample': nrm((DEC_BATCH, DEC_SEQ, D), 1.0),
        'cache_a_k': nrm((N_A, DEC_BATCH, win_rows, N_KV_HEADS, HEAD_DIM), 1.0),
        'cache_a_v': nrm((N_A, DEC_BATCH, win_rows, N_KV_HEADS, HEAD_DIM), 1.0),
        'state_b_wkv': nrm((N_B, DEC_BATCH, RWKV_HEADS, RWKV_HEAD, RWKV_HEAD), 0.5),
        'state_b_shift': nrm((N_B, DEC_BATCH, D), 1.0),
        'state_c_conv': nrm((N_C, DEC_BATCH, CONV_WIDTH - 1, D), 1.0),
        'rel_bias_table': nrm((NUM_BUCKETS, N_HEADS), 0.5),
        'norm_g': 1.0 + nrm((DEPTH, 4, D), 0.05),
        'a_w_qkv': nrm((N_A, D, d_qkv), D ** -0.5),
        'a_w_o': nrm((N_A, N_HEADS * HEAD_DIM, D), (N_HEADS * HEAD_DIM) ** -0.5),
        'a_sinks': nrm((N_A, N_HEADS), 1.0),
        'b_mu': jax.random.uniform(next(ks), (N_B, 6, D), F32),
        'b_w_rkv': nrm((N_B, 3, D, D), D ** -0.5),
        'b_w_o': nrm((N_B, D, D), D ** -0.5),
        'b_w0': nrm((N_B, D), 0.5) - 1.0,
        'b_w1': nrm((N_B, D, D_DECAY_LORA), D ** -0.5),
        'b_w2': nrm((N_B, D_DECAY_LORA, D), 0.5 * D_DECAY_LORA ** -0.5),
        'b_a0': nrm((N_B, D), 0.1),
        'b_a1': nrm((N_B, D, D_AAA_LORA), D ** -0.5),
        'b_a2': nrm((N_B, D_AAA_LORA, D), 0.5 * D_AAA_LORA ** -0.5),
        'b_g1': nrm((N_B, D, D_GATE_LORA), D ** -0.5),
        'b_g2': nrm((N_B, D_GATE_LORA, D), D_GATE_LORA ** -0.5),
        'b_k_k': 0.85 + nrm((N_B, D), 0.05),
        'b_k_a': 1.0 + nrm((N_B, D), 0.05),
        'b_r_k': nrm((N_B, RWKV_HEADS, RWKV_HEAD), 0.1),
        'b_ln_w': 1.0 + nrm((N_B, D), 0.05),
        'b_ln_b': nrm((N_B, D), 0.02),
        'c_w_in': nrm((N_C, D, 3 * D), D ** -0.5),
        'c_conv_w': nrm((N_C, CONV_WIDTH, D), CONV_WIDTH ** -0.5),
        'c_w_out': nrm((N_C, D, D), D ** -0.5),
        'mlp_w1': nrm((DEPTH, D, D_FF), D ** -0.5),
        'mlp_w2': nrm((DEPTH, D_FF, D), D_FF ** -0.5),
    }


def reference(x_prompt, x_sample, cache_a_k, cache_a_v, state_b_wkv, state_b_shift, state_c_conv,
              rel_bias_table, norm_g, a_w_qkv, a_w_o, a_sinks,
              b_mu, b_w_rkv, b_w_o, b_w0, b_w1, b_w2, b_a0, b_a1, b_a2, b_g1, b_g2,
              b_k_k, b_k_a, b_r_k, b_ln_w, b_ln_b,
              c_w_in, c_conv_w, c_w_out, mlp_w1, mlp_w2):
    P = dict(rel_bias_table=rel_bias_table, norm_g=norm_g, a_w_qkv=a_w_qkv, a_w_o=a_w_o, a_sinks=a_sinks,
             b_mu=b_mu, b_w_rkv=b_w_rkv, b_w_o=b_w_o, b_w0=b_w0, b_w1=b_w1, b_w2=b_w2,
             b_a0=b_a0, b_a1=b_a1, b_a2=b_a2, b_g1=b_g1, b_g2=b_g2, b_k_k=b_k_k, b_k_a=b_k_a,
             b_r_k=b_r_k, b_ln_w=b_ln_w, b_ln_b=b_ln_b, c_w_in=c_w_in, c_conv_w=c_conv_w,
             c_w_out=c_w_out, mlp_w1=mlp_w1, mlp_w2=mlp_w2)
    y_prompt, ak_p, av_p, wkv_p, sh_p, cv_p = _trunk(x_prompt, True, None, None, None, None, None, P)
    y_sample, ak_s, av_s, wkv_s, sh_s, cv_s = _trunk(x_sample, False, cache_a_k, cache_a_v,
                                                     state_b_wkv, state_b_shift, state_c_conv, P)
    return (y_prompt, y_sample, ak_p, av_p, ak_s, av_s, wkv_p, wkv_s, sh_p, sh_s, cv_p, cv_s)
```

```python
import functools
import math

import jax
import jax.numpy as jnp
from jax import lax
from jax.experimental import pallas as pl
from jax.experimental.pallas import tpu as pltpu

F32 = jnp.float32
BF16 = jnp.bfloat16

HEAD_DIM = 64
CHUNK = 64
WINDOW = 128
NUM_BUCKETS = 32
MAX_DISTANCE = 128
RWKV_HEAD = 64
CONV_WIDTH = 3
N_MIXERS = 3
RMS_EPS = 1e-6
GN_EPS = RWKV_HEAD * 1e-5

LANES = 128
VMEM_LIMIT = 56 * 1024 * 1024


def _cparams(*sem):
    return pltpu.CompilerParams(dimension_semantics=sem, vmem_limit_bytes=VMEM_LIMIT)


def _resident(shape):
    nd = len(shape)
    return pl.BlockSpec(shape, lambda *_: (0,) * nd, pipeline_mode=pl.Buffered(1))


def _rms(x, g):
    return x * lax.rsqrt(jnp.mean(x * x, axis=-1, keepdims=True) + RMS_EPS) * g


def _dot(a, b):
    return jnp.dot(a, b, preferred_element_type=F32)


def _dot_nt(a, b):
    return lax.dot_general(a, b, (((1,), (1,)), ((), ())), preferred_element_type=F32)


def _dot_tn(a, b):
    return lax.dot_general(a, b, (((0,), (0,)), ((), ())), preferred_element_type=F32)


def _sigmoid(x):
    return 1.0 / (1.0 + jnp.exp(-x))


def _post_kernel(x_ref, z_ref, wo_ref, g_ref, w1_ref, w2_ref, o_ref, acc_ref, *, ff_chunk):
    g = g_ref[...]
    m = _dot(z_ref[...], wo_ref[...])
    x1 = x_ref[...] + _rms(m, g[1:2])
    h2 = _rms(x1, g[2:3]).astype(BF16)
    d_ff = w1_ref.shape[1]
    for c in range(d_ff // ff_chunk):
        sl = slice(c * ff_chunk, (c + 1) * ff_chunk)
        a = jnp.maximum(_dot(h2, w1_ref[:, sl]), 0.0)
        part = _dot((a * a).astype(BF16), w2_ref[sl, :])
        if c == 0:
            acc_ref[...] = part
        else:
            acc_ref[...] += part
    o_ref[...] = x1 + _rms(acc_ref[...], g[3:4])


def _post(x, z, w_out, g4, w1, w2, *, tm):
    M, D = x.shape
    d_ff = w1.shape[1]
    row = lambda i: (i, 0)
    return pl.pallas_call(
        functools.partial(_post_kernel, ff_chunk=min(d_ff, 1024)),
        grid=(M // tm,),
        in_specs=[pl.BlockSpec((tm, D), row), pl.BlockSpec((tm, D), row),
                  _resident((D, D)), _resident((4, D)), _resident((D, d_ff)), _resident((d_ff, D))],
        out_specs=pl.BlockSpec((tm, D), row),
        out_shape=jax.ShapeDtypeStruct((M, D), F32),
        scratch_shapes=[pltpu.VMEM((tm, D), F32)],
        compiler_params=_cparams("parallel"),
        name="post_mlp",
    )(x, z, w_out, g4, w1, w2)


def _qkv_kernel(x_ref, g_ref, w_ref, q_ref, kv_ref, tail_ref, *, tiles_per_seq, tail):
    h = _rms(x_ref[...], g_ref[0:1, :]).astype(BF16)
    p = _dot(h, w_ref[...])
    D = q_ref.shape[1]
    q_ref[...] = p[:, :D].astype(BF16)
    kv_ref[...] = p[:, D:].astype(BF16)
    tm = p.shape[0]

    @pl.when(pl.program_id(0) % tiles_per_seq == tiles_per_seq - 1)
    def _():
        tail_ref[0] = p[tm - tail:, D:]


def _qkv(x, g4, w, *, seq, tm):
    M, D = x.shape
    KV = w.shape[1] - D
    tail = min(WINDOW, seq)
    assert seq % tm == 0 and tail <= tm
    tps = seq // tm
    row = lambda i: (i, 0)
    return pl.pallas_call(
        functools.partial(_qkv_kernel, tiles_per_seq=tps, tail=tail),
        grid=(M // tm,),
        in_specs=[pl.BlockSpec((tm, D), row), _resident((4, D)), _resident(w.shape)],
        out_specs=[pl.BlockSpec((tm, D), row), pl.BlockSpec((tm, KV), row),
                   pl.BlockSpec((1, tail, KV), lambda i: (i // tps, 0, 0))],
        out_shape=[jax.ShapeDtypeStruct((M, D), BF16), jax.ShapeDtypeStruct((M, KV), BF16),
                   jax.ShapeDtypeStruct((M // seq, tail, KV), F32)],
        compiler_params=_cparams("arbitrary"),
        name="attn_qkv",
    )(x, g4, w)


def _attn_kernel(sink_ref, q_ref, prev_ref, cur_ref, bias_ref, o_ref, kvx_ref, *, qc, masked):
    tq, D = q_ref.shape
    kvw = cur_ref.shape[1] // 2
    band = WINDOW + qc
    kvx_ref[0:WINDOW, :] = prev_ref[...]
    kvx_ref[WINDOW:, :] = cur_ref[...]
    lane = lax.broadcasted_iota(jnp.int32, (1, LANES), 1)
    low = lane < HEAD_DIM
    row2 = lax.broadcasted_iota(jnp.int32, (2 * qc, 1), 0)
    keypos = lax.broadcasted_iota(jnp.int32, (1, band), 1)
    tile_start = pl.program_id(1) * tq
    scale = HEAD_DIM ** -0.5
    n_pairs = D // LANES
    pairs_per_kv = n_pairs // (kvw // LANES)

    def chunk(c, carry):
        r0 = pl.multiple_of(c * qc, qc)
        qrows = q_ref[pl.ds(r0, qc), :]
        kvb = kvx_ref[pl.ds(r0, band), :]
        if masked:
            valid = (tile_start + r0 - WINDOW + keypos) >= 0
        for p in range(n_pairs):
            qp = qrows[:, p * LANES:(p + 1) * LANES]
            zero = jnp.zeros_like(qp)
            lhs = jnp.concatenate([jnp.where(low, qp, zero), jnp.where(low, zero, qp)], axis=0)
            hk = p // pairs_per_kv
            kb = kvb[:, hk * LANES:(hk + 1) * LANES]
            vb = kvb[:, kvw + hk * LANES: kvw + (hk + 1) * LANES]
            s = _dot_nt(lhs, kb) * scale + bias_ref[p]
            if masked:
                s = jnp.where(valid, s, -jnp.inf)
            sink = jnp.where(row2 < qc, sink_ref[2 * p], sink_ref[2 * p + 1])
            m = jnp.maximum(jnp.max(s, axis=-1, keepdims=True), sink)
            e = jnp.exp(s - m)
            den = jnp.sum(e, axis=-1, keepdims=True) + jnp.exp(sink - m)
            o2 = _dot(e.astype(BF16), vb) / den
            o_ref[pl.ds(r0, qc), p * LANES:(p + 1) * LANES] = jnp.where(low, o2[:qc], o2[qc:]).astype(BF16)
        return carry

    lax.fori_loop(0, tq // qc, chunk, 0)


def _attn(q, kv, prev, prev_map, bias, sinks, *, nseq, seq, tq, qc, masked):
    M, D = q.shape
    KV2 = kv.shape[1]
    npt = seq // tq
    row = lambda b, i: (b * npt + i, 0)
    return pl.pallas_call(
        functools.partial(_attn_kernel, qc=qc, masked=masked),
        grid=(nseq, npt),
        in_specs=[pl.BlockSpec(memory_space=pltpu.SMEM),
                  pl.BlockSpec((tq, D), row),
                  pl.BlockSpec((WINDOW, KV2), prev_map),
                  pl.BlockSpec((tq, KV2), row),
                  _resident(bias.shape)],
        out_specs=pl.BlockSpec((tq, D), row),
        out_shape=jax.ShapeDtypeStruct((M, D), BF16),
        scratch_shapes=[pltpu.VMEM((WINDOW + tq, KV2), BF16)],
        compiler_params=_cparams("parallel", "parallel"),
        name="attn_core",
    )(sinks, q, prev, kv, bias)


def _rel_bucket(rp):
    nb = NUM_BUCKETS // 2
    ret = (rp > 0).astype(jnp.int32) * nb
    n = jnp.abs(rp)
    max_exact = nb // 2
    nf = jnp.maximum(n, 1).astype(F32)
    large = max_exact + (jnp.log(nf / max_exact) / math.log(MAX_DISTANCE / max_exact)
                         * (nb - max_exact)).astype(jnp.int32)
    large = jnp.minimum(large, nb - 1)
    return ret + jnp.where(n < max_exact, n, large)


def _pair_bias(table, q_len, k_len, k_offset):
    i = jnp.arange(q_len)[:, None]
    j = jnp.arange(k_len)[None, :]
    b = table[_rel_bucket(j - k_offset - i)]
    return jnp.transpose(b, (2, 0, 1)).reshape(table.shape[1] // 2, 2 * q_len, k_len)


def _dup_heads(t, n_kv):
    lead = t.shape[:-1]
    t = t.reshape(lead + (n_kv, 1, HEAD_DIM))
    return jnp.broadcast_to(t, lead + (n_kv, 2, HEAD_DIM)).reshape(lead + (n_kv * LANES,))


def _conv_kernel(x_ref, g_ref, w_ref, cw_ref, u0_ref, z_ref, un_ref, carry_ref):
    @pl.when(pl.program_id(1) == 0)
    def _():
        carry_ref[...] = u0_ref[0]

    D = x_ref.shape[1]
    h = _rms(x_ref[...], g_ref[0:1, :]).astype(BF16)
    p = _dot(h, w_ref[...])
    bg, u = p[:, :D], p[:, D:2 * D] * p[:, 2 * D:]
    tm = u.shape[0]
    row = lax.broadcasted_iota(jnp.int32, (tm, 1), 0)
    c0, c1 = carry_ref[0:1, :], carry_ref[1:2, :]
    um1 = jnp.where(row == 0, c1, pltpu.roll(u, 1, axis=0))
    um2 = jnp.where(row == 0, c0, jnp.where(row == 1, c1, pltpu.roll(u, 2, axis=0)))
    cw = cw_ref[...]
    y = um2 * cw[0:1] + um1 * cw[1:2] + u * cw[2:3]
    z_ref[...] = (bg * y).astype(BF16)
    carry_ref[...] = u[tm - 2:, :]
    un_ref[0] = u[tm - 2:, :]


def _conv(x, g4, w_in, cw, u0, *, nseq, seq, tm):
    M, D = x.shape
    npt = seq // tm
    row = lambda b, i: (b * npt + i, 0)
    st = lambda b, i: (b, 0, 0)
    return pl.pallas_call(
        _conv_kernel,
        grid=(nseq, npt),
        in_specs=[pl.BlockSpec((tm, D), row), _resident((4, D)), _resident(w_in.shape),
                  _resident(cw.shape), pl.BlockSpec((1, CONV_WIDTH - 1, D), st)],
        out_specs=[pl.BlockSpec((tm, D), row), pl.BlockSpec((1, CONV_WIDTH - 1, D), st)],
        out_shape=[jax.ShapeDtypeStruct((M, D), BF16),
                   jax.ShapeDtypeStruct((nseq, CONV_WIDTH - 1, D), F32)],
        scratch_shapes=[pltpu.VMEM((CONV_WIDTH - 1, D), F32)],
        compiler_params=_cparams("arbitrary", "arbitrary"),
        name="conv_mix",
    )(x, g4, w_in, cw, u0)


def _rwkv_proj_kernel(x_ref, g_ref, mu_ref, wrkv_ref, vec_ref, w1_ref, w2_ref, a1_ref, a2_ref,
                      g1_ref, g2_ref, sh0_ref,
                      r_ref, k_ref, v_ref, lw_ref, a_ref, gate_ref, sh_ref, carry_ref):
    @pl.when(pl.program_id(1) == 0)
    def _():
        carry_ref[...] = sh0_ref[0]

    h = _rms(x_ref[...], g_ref[0:1, :])
    tm = h.shape[0]
    row = lax.broadcasted_iota(jnp.int32, (tm, 1), 0)
    xx = jnp.where(row == 0, carry_ref[...], pltpu.roll(h, 1, axis=0)) - h
    mu = mu_ref[...]
    mix = lambda i: (h + xx * mu[i:i + 1]).astype(BF16)
    r_ref[...] = _dot(mix(0), wrkv_ref[0]).astype(BF16)
    k_ref[...] = _dot(mix(2), wrkv_ref[1])
    v_ref[...] = _dot(mix(3), wrkv_ref[2]).astype(BF16)
    vec = vec_ref[...]
    wl = vec[0:1] + _dot(jnp.tanh(_dot(mix(1), w1_ref[...])).astype(BF16), w2_ref[...])
    t = -wl
    sp = jnp.maximum(t, 0.0) + jnp.log1p(jnp.exp(-jnp.abs(t)))
    lw_ref[...] = -jnp.exp(-sp - 0.5)
    a_ref[...] = _sigmoid(vec[1:2] + _dot(_dot(mix(4), a1_ref[...]).astype(BF16), a2_ref[...]))
    gate_ref[...] = _dot(_sigmoid(_dot(mix(5), g1_ref[...])).astype(BF16), g2_ref[...]).astype(BF16)
    carry_ref[...] = h[tm - 1:, :]
    sh_ref[0] = h[tm - 1:, :]


def _rwkv_proj(x, g4, mu, wrkv, vec, w1, w2, a1, a2, g1, g2, sh0, *, nseq, seq, tm):
    M, D = x.shape
    npt = seq // tm
    row = lambda b, i: (b * npt + i, 0)
    st = lambda b, i: (b, 0, 0)
    big = pl.BlockSpec((tm, D), row)
    sd = lambda dt: jax.ShapeDtypeStruct((M, D), dt)
    return pl.pallas_call(
        _rwkv_proj_kernel,
        grid=(nseq, npt),
        in_specs=[big, _resident((4, D)), _resident(mu.shape), _resident(wrkv.shape), _resident(vec.shape),
                  _resident(w1.shape), _resident(w2.shape), _resident(a1.shape), _resident(a2.shape),
                  _resident(g1.shape), _resident(g2.shape), pl.BlockSpec((1, 1, D), st)],
        out_specs=[big, big, big, big, big, big, pl.BlockSpec((1, 1, D), st)],
        out_shape=[sd(BF16), sd(F32), sd(BF16), sd(F32), sd(F32), sd(BF16),
                   jax.ShapeDtypeStruct((nseq, 1, D), F32)],
        scratch_shapes=[pltpu.VMEM((1, D), F32)],
        compiler_params=_cparams("arbitrary", "arbitrary"),
        name="rwkv_proj",
    )(x, g4, mu, wrkv, vec, w1, w2, a1, a2, g1, g2, sh0)


def _split_rows(x):
    hi = x.astype(BF16)
    lo = (x - hi.astype(F32)).astype(BF16)
    return jnp.concatenate([hi, lo], axis=0)


def _wkv_chunk(r, k, v, lw, a, gate, S, prm):
    C = r.shape[0]
    C2 = 2 * C
    H = RWKV_HEAD
    lane = lax.broadcasted_iota(jnp.int32, (1, LANES), 1)
    low = lane < H
    hr = lax.broadcasted_iota(jnp.int32, (LANES, LANES), 0)
    hc = lax.broadcasted_iota(jnp.int32, (LANES, LANES), 1)
    head_bd = (hr < H) == (hc < H)
    ones_bd = jnp.where(head_bd, 1.0, 0.0).astype(BF16)

    def segsum(x):
        s = _dot(_split_rows(x), ones_bd)
        return s[:C] + s[C:]

    def stack(x):
        z = jnp.zeros_like(x)
        return jnp.concatenate([jnp.where(low, x, z), jnp.where(low, z, x)], axis=0)

    tr = lax.broadcasted_iota(jnp.int32, (C, C), 0)
    tc = lax.broadcasted_iota(jnp.int32, (C, C), 1)
    tri = jnp.where(tr >= tc, 1.0, 0.0).astype(BF16)
    lw_hi = lw.astype(BF16)
    lw_lo = (lw - lw_hi.astype(F32)).astype(BF16)
    cum2 = _dot(tri, jnp.concatenate([lw_hi, lw_lo], axis=1))
    cum = cum2[:, :LANES] + cum2[:, LANES:]
    G = jnp.exp(cum)
    Gm1 = jnp.exp(cum - lw)
    iG = jnp.exp(-cum)
    cum_end = cum[C - 1:C, :]
    G_end = jnp.exp(cum_end)
    G_rest = jnp.exp(cum_end - cum)

    kkr = k * prm[0:1]
    kk = kkr * lax.rsqrt(jnp.maximum(segsum(kkr * kkr), 1e-24))
    k2 = k * (1.0 + (a - 1.0) * prm[1:2])
    b = kk * a
    At = -kk * Gm1
    Rt = r * G

    rr = lax.broadcasted_iota(jnp.int32, (C2, C2), 0)
    cc = lax.broadcasted_iota(jnp.int32, (C2, C2), 1)
    same = (rr >= C) == (cc >= C)
    strict = same & (rr > cc)
    incl = same & (rr >= cc)

    lhs = jnp.concatenate([stack(At), stack(Rt)], axis=0).astype(BF16)
    Bt = (b * iG).astype(BF16)
    Kt = (k2 * iG).astype(BF16)
    P = _dot_nt(lhs, jnp.concatenate([Bt, Bt, Kt, Kt], axis=0))
    Lab = jnp.where(strict, P[:C2, :C2], 0.0)
    fold = lambda x: x[:C] + x[C:]
    Lak = fold(jnp.where(strict, P[:C2, C2:], 0.0))
    Lrb = fold(jnp.where(incl, P[C2:, :C2], 0.0))
    Lrk = fold(jnp.where(incl, P[C2:, C2:], 0.0))

    blk = lambda s: (rr >> int(math.log2(s))) == (cc >> int(math.log2(s)))
    eye = jnp.where(rr == cc, 1.0, 0.0)
    L1 = jnp.where(blk(8), Lab, 0.0)
    L1b = L1.astype(BF16)
    L2 = _dot(L1b, L1b)
    L2b = L2.astype(BF16)
    L4 = _dot(L2b, L2b)
    T = _dot((eye + L1).astype(BF16), (eye + L2).astype(BF16))
    T = _dot(T.astype(BF16), (eye + L4).astype(BF16))
    s = 8
    while s < C:
        Mx = jnp.where(blk(2 * s) & jnp.logical_not(blk(s)), Lab, 0.0).astype(BF16)
        Tb = T.astype(BF16)
        T = T + _dot(_dot(Tb, Mx).astype(BF16), Tb)
        s *= 2

    Sb = S.astype(BF16)
    AZ = _dot_nt(jnp.concatenate([At, Rt], axis=0).astype(BF16), Sb)
    Vst = stack(v).astype(BF16)
    rhs = AZ[:C] + _dot(Lak.astype(BF16), Vst)
    Ust = _dot(T.astype(BF16), stack(rhs).astype(BF16))
    U = fold(Ust)
    Y = AZ[C:] + _dot(jnp.concatenate([Lrb, Lrk], axis=1).astype(BF16),
                      jnp.concatenate([Ust.astype(BF16), Vst], axis=0))
    upd = _dot_tn(jnp.concatenate([U, v], axis=0).astype(BF16),
                  jnp.concatenate([b * G_rest, k2 * G_rest], axis=0).astype(BF16))
    S_new = S * G_end + jnp.where(head_bd, upd, 0.0)

    mean = segsum(Y) * (1.0 / H)
    d = Y - mean
    var = segsum(d * d) * (1.0 / H)
    yn = d * lax.rsqrt(var + GN_EPS) * prm[3:4] + prm[4:5]
    bonus = segsum(r * k2 * prm[2:3]) * v
    return (yn + bonus) * gate, S_new


def _wkv_kernel(r_ref, k_ref, v_ref, lw_ref, a_ref, g_ref, prm_ref, s0_ref, z_ref, sf_ref, *, C):
    @pl.when(pl.program_id(1) == 0)
    def _():
        sf_ref[...] = s0_ref[...]

    tb, D = r_ref.shape
    n_pairs = D // LANES

    def chunk(c, carry):
        r0 = pl.multiple_of(c * C, C)
        rows = pl.ds(r0, C)
        for p in range(n_pairs):
            ln = slice(p * LANES, (p + 1) * LANES)
            z, s_new = _wkv_chunk(r_ref[rows, ln].astype(F32), k_ref[rows, ln], v_ref[rows, ln].astype(F32),
                                  lw_ref[rows, ln], a_ref[rows, ln], g_ref[rows, ln].astype(F32),
                                  sf_ref[0, p], prm_ref[:, ln])
            z_ref[rows, ln] = z.astype(BF16)
            sf_ref[0, p] = s_new
        return carry

    lax.fori_loop(0, tb // C, chunk, 0)


def _wkv(r, k, v, lw, a, gate, prm, s0, *, nseq, seq, tb, C):
    M, D = r.shape
    npt = seq // tb
    row = lambda b, i: (b * npt + i, 0)
    big = pl.BlockSpec((tb, D), row)
    st = pl.BlockSpec((1,) + s0.shape[1:], lambda b, i: (b, 0, 0, 0))
    return pl.pallas_call(
        functools.partial(_wkv_kernel, C=C),
        grid=(nseq, npt),
        in_specs=[big, big, big, big, big, big, _resident(prm.shape), st],
        out_specs=[big, st],
        out_shape=[jax.ShapeDtypeStruct((M, D), BF16), jax.ShapeDtypeStruct(s0.shape, F32)],
        compiler_params=_cparams("arbitrary", "arbitrary"),
        name="rwkv_wkv",
    )(r, k, v, lw, a, gate, prm, s0)


def _pair_state(s):
    B, Hh, N, _ = s.shape
    s = s.reshape(B, Hh // 2, 2, N, N)
    z = jnp.zeros((B, Hh // 2, 2, N, 2, N), s.dtype)
    z = z.at[:, :, 0, :, 0, :].set(s[:, :, 0]).at[:, :, 1, :, 1, :].set(s[:, :, 1])
    return z.reshape(B, Hh // 2, 2 * N, 2 * N)


def _unpair_state(z):
    B, Hp, N2, _ = z.shape
    N = N2 // 2
    z = z.reshape(B, Hp, 2, N, 2, N)
    return jnp.stack([z[:, :, 0, :, 0, :], z[:, :, 1, :, 1, :]], axis=2).reshape(B, 2 * Hp, N, N)


def _tile(n, pref):
    t = min(n, pref)
    assert n % t == 0
    return t


def _trunk(x3, prompt, a_k, a_v, b_wkv, b_shift, c_conv, W):
    B, T, D = x3.shape
    M = B * T
    x = x3.reshape(M, D)
    depth = W['norm_g'].shape[0]
    n_heads = D // HEAD_DIM
    n_kv = a_k.shape[3] if a_k is not None else W['n_kv']
    nk, nv, nwkv, nsh, ncv = [], [], [], [], []
    seq_tile = _tile(T, 512)
    for i in range(depth):
        kind, j = i % N_MIXERS, i // N_MIXERS
        g4 = W['norm_g'][i]
        if kind == 0:
            q, kv, tail = _qkv(x, g4, W['a_w_qkv'][j], seq=T, tm=seq_tile)
            kvw = n_kv * LANES
            tail_k = tail[:, :, :kvw].reshape(B, -1, n_kv, 2, HEAD_DIM)[:, :, :, 0]
            tail_v = tail[:, :, kvw:].reshape(B, -1, n_kv, 2, HEAD_DIM)[:, :, :, 0]
            if prompt:
                qc = CHUNK
                npt128 = T // WINDOW
                step128 = seq_tile // WINDOW
                prev = kv
                prev_map = lambda b, t, n=npt128, s=step128: (jnp.maximum(b * n + t * s - 1, 0), 0)
                k_new, v_new = tail_k, tail_v
            else:
                qc = T
                ck, cv = a_k[j], a_v[j]
                prev = jnp.concatenate([_dup_heads(ck.reshape(B, WINDOW, -1), n_kv),
                                        _dup_heads(cv.reshape(B, WINDOW, -1), n_kv)],
                                       axis=-1).astype(BF16).reshape(B * WINDOW, 2 * kvw)
                prev_map = lambda b, t: (b, 0)
                k_new = jnp.concatenate([ck, tail_k], axis=1)[:, -WINDOW:]
                v_new = jnp.concatenate([cv, tail_v], axis=1)[:, -WINDOW:]
            bias = _pair_bias(W['rel_bias_table'], qc, WINDOW + qc, WINDOW)
            z = _attn(q, kv, prev, prev_map, bias, W['a_sinks'][j], nseq=B, seq=T, tq=seq_tile, qc=qc,
                      masked=prompt)
            nk.append(k_new)
            nv.append(v_new)
            w_out = W['a_w_o'][j]
        elif kind == 1:
            if prompt:
                sh0 = jnp.zeros((B, 1, D), F32)
                s0 = jnp.zeros((B, n_heads // 2, LANES, LANES), F32)
            else:
                sh0 = b_shift[j].reshape(B, 1, D)
                s0 = _pair_state(b_wkv[j])
            r, k, v, lw, a, gate, sh_new = _rwkv_proj(
                x, g4, W['b_mu'][j], W['b_w_rkv'][j], W['b_vec'][j], W['b_w1'][j], W['b_w2'][j],
                W['b_a1'][j], W['b_a2'][j], W['b_g1'][j], W['b_g2'][j], sh0,
                nseq=B, seq=T, tm=_tile(T, 256))
            z, s_fin = _wkv(r, k, v, lw, a, gate, W['b_prm'][j], s0, nseq=B, seq=T,
                            tb=_tile(T, 256), C=min(T, CHUNK))
            nwkv.append(_unpair_state(s_fin))
            nsh.append(sh_new.reshape(B, D))
            w_out = W['b_w_o'][j]
        else:
            u0 = jnp.zeros((B, CONV_WIDTH - 1, D), F32) if prompt else c_conv[j]
            z, u_new = _conv(x, g4, W['c_w_in'][j], W['c_conv_w'][j], u0, nseq=B, seq=T, tm=seq_tile)
            ncv.append(u_new)
            w_out = W['c_w_out'][j]
        x = _post(x, z, w_out, g4, W['mlp_w1'][i], W['mlp_w2'][i], tm=_tile(M, 512))
    return (x.reshape(B, T, D), jnp.stack(nk), jnp.stack(nv), jnp.stack(nwkv), jnp.stack(nsh),
            jnp.stack(ncv))


def kernel(x_prompt, x_sample, cache_a_k, cache_a_v, state_b_wkv, state_b_shift, state_c_conv, rel_bias_table, norm_g, a_w_qkv, a_w_o, a_sinks, b_mu, b_w_rkv, b_w_o, b_w0, b_w1, b_w2, b_a0, b_a1, b_a2, b_g1, b_g2, b_k_k, b_k_a, b_r_k, b_ln_w, b_ln_b, c_w_in, c_conv_w, c_w_out, mlp_w1, mlp_w2):
    D = x_prompt.shape[-1]
    n_kv = cache_a_k.shape[3]
    nq = D
    nkv = n_kv * HEAD_DIM
    bf = lambda t: t.astype(BF16)
    wq, wk, wv = a_w_qkv[..., :nq], a_w_qkv[..., nq:nq + nkv], a_w_qkv[..., nq + nkv:]
    w_qkv = bf(jnp.concatenate([wq, _dup_heads(wk, n_kv), _dup_heads(wv, n_kv)], axis=-1))
    n_b = b_mu.shape[0]
    zeros = jnp.zeros_like(b_w0)
    W = dict(
        n_kv=n_kv, rel_bias_table=rel_bias_table, norm_g=norm_g, a_w_qkv=w_qkv, a_w_o=bf(a_w_o), a_sinks=a_sinks,
        b_mu=b_mu, b_w_rkv=bf(b_w_rkv), b_w_o=bf(b_w_o),
        b_vec=jnp.stack([b_w0, b_a0], axis=1),
        b_w1=bf(b_w1), b_w2=bf(b_w2), b_a1=bf(b_a1), b_a2=bf(b_a2), b_g1=bf(b_g1), b_g2=bf(b_g2),
        b_prm=jnp.stack([b_k_k, b_k_a, b_r_k.reshape(n_b, D), b_ln_w, b_ln_b, zeros, zeros, zeros], axis=1),
        c_w_in=bf(c_w_in), c_conv_w=c_conv_w, c_w_out=bf(c_w_out), mlp_w1=bf(mlp_w1), mlp_w2=bf(mlp_w2))
    y_p, ak_p, av_p, wkv_p, sh_p, cv_p = _trunk(x_prompt, True, None, None, None, None, None, W)
    y_s, ak_s, av_s, wkv_s, sh_s, cv_s = _trunk(x_sample, False, cache_a_k, cache_a_v, state_b_wkv,
                                                state_b_shift, state_c_conv, W)
    return (y_p, y_s, ak_p, av_p, ak_s, av_s, wkv_p, wkv_s, sh_p, sh_s, cv_p, cv_s)
```

```python
import functools
import math

import jax
import jax.numpy as jnp
from jax import lax
from jax.experimental import pallas as pl
from jax.experimental.pallas import tpu as pltpu

F32 = jnp.float32
BF16 = jnp.bfloat16

HEAD_DIM = 64
CHUNK = 64
WINDOW = 128
NUM_BUCKETS = 32
MAX_DISTANCE = 128
RWKV_HEAD = 64
CONV_WIDTH = 3
N_MIXERS = 3
RMS_EPS = 1e-6
GN_EPS = RWKV_HEAD * 1e-5

LANES = 128
VMEM_LIMIT = 56 * 1024 * 1024


def _cparams(*sem):
    return pltpu.CompilerParams(dimension_semantics=sem, vmem_limit_bytes=VMEM_LIMIT)


def _resident(shape):
    nd = len(shape)
    return pl.BlockSpec(shape, lambda *_: (0,) * nd, pipeline_mode=pl.Buffered(1))


def _rms(x, g):
    return x * lax.rsqrt(jnp.mean(x * x, axis=-1, keepdims=True) + RMS_EPS) * g


def _dot(a, b):
    return jnp.dot(a, b, preferred_element_type=F32)


def _dot_nt(a, b):
    return lax.dot_general(a, b, (((1,), (1,)), ((), ())), preferred_element_type=F32)


def _dot_tn(a, b):
    return lax.dot_general(a, b, (((0,), (0,)), ((), ())), preferred_element_type=F32)


def _sigmoid(x):
    return 1.0 / (1.0 + jnp.exp(-x))


def _post_kernel(x_ref, z_ref, wo_ref, g_ref, w1_ref, w2_ref, o_ref, acc_ref, *, ff_chunk):
    g = g_ref[...]
    m = _dot(z_ref[...], wo_ref[...])
    x1 = x_ref[...] + _rms(m, g[1:2])
    h2 = _rms(x1, g[2:3]).astype(BF16)
    d_ff = w1_ref.shape[1]
    for c in range(d_ff // ff_chunk):
        sl = slice(c * ff_chunk, (c + 1) * ff_chunk)
        a = jnp.maximum(_dot(h2, w1_ref[:, sl]), 0.0)
        part = _dot((a * a).astype(BF16), w2_ref[sl, :])
        if c == 0:
            acc_ref[...] = part
        else:
            acc_ref[...] += part
    o_ref[...] = x1 + _rms(acc_ref[...], g[3:4])


def _post(x, z, w_out, g4, w1, w2, *, tm):
    M, D = x.shape
    d_ff = w1.shape[1]
    row = lambda i: (i, 0)
    return pl.pallas_call(
        functools.partial(_post_kernel, ff_chunk=min(d_ff, 1024)),
        grid=(M // tm,),
        in_specs=[pl.BlockSpec((tm, D), row), pl.BlockSpec((tm, D), row),
                  _resident((D, D)), _resident((4, D)), _resident((D, d_ff)), _resident((d_ff, D))],
        out_specs=pl.BlockSpec((tm, D), row),
        out_shape=jax.ShapeDtypeStruct((M, D), F32),
        scratch_shapes=[pltpu.VMEM((tm, D), F32)],
        compiler_params=_cparams("parallel"),
        name="post_mlp",
    )(x, z, w_out, g4, w1, w2)


def _qkv_kernel(x_ref, g_ref, w_ref, q_ref, kv_ref, tail_ref, *, tiles_per_seq, tail):
    h = _rms(x_ref[...], g_ref[0:1, :]).astype(BF16)
    p = _dot(h, w_ref[...])
    D = q_ref.shape[1]
    q_ref[...] = p[:, :D].astype(BF16)
    kv_ref[...] = p[:, D:].astype(BF16)
    tm = p.shape[0]

    @pl.when(pl.program_id(0) % tiles_per_seq == tiles_per_seq - 1)
    def _():
        tail_ref[0] = p[tm - tail:, D:]


def _qkv(x, g4, w, *, seq, tm):
    M, D = x.shape
    KV = w.shape[1] - D
    tail = min(WINDOW, seq)
    assert seq % tm == 0 and tail <= tm
    tps = seq // tm
    row = lambda i: (i, 0)
    return pl.pallas_call(
        functools.partial(_qkv_kernel, tiles_per_seq=tps, tail=tail),
        grid=(M // tm,),
        in_specs=[pl.BlockSpec((tm, D), row), _resident((4, D)), _resident(w.shape)],
        out_specs=[pl.BlockSpec((tm, D), row), pl.BlockSpec((tm, KV), row),
                   pl.BlockSpec((1, tail, KV), lambda i: (i // tps, 0, 0))],
        out_shape=[jax.ShapeDtypeStruct((M, D), BF16), jax.ShapeDtypeStruct((M, KV), BF16),
                   jax.ShapeDtypeStruct((M // seq, tail, KV), F32)],
        compiler_params=_cparams("arbitrary"),
        name="attn_qkv",
    )(x, g4, w)


def _attn_kernel(sink_ref, q_ref, prev_ref, cur_ref, bias_ref, o_ref, kvx_ref, *, qc, masked):
    tq, D = q_ref.shape
    kvw = cur_ref.shape[1] // 2
    band = WINDOW + qc
    kvx_ref[0:WINDOW, :] = prev_ref[...]
    kvx_ref[WINDOW:, :] = cur_ref[...]
    lane = lax.broadcasted_iota(jnp.int32, (1, LANES), 1)
    low = lane < HEAD_DIM
    row2 = lax.broadcasted_iota(jnp.int32, (2 * qc, 1), 0)
    keypos = lax.broadcasted_iota(jnp.int32, (1, band), 1)
    tile_start = pl.program_id(1) * tq
    scale = HEAD_DIM ** -0.5
    n_pairs = D // LANES
    pairs_per_kv = n_pairs // (kvw // LANES)

    def chunk(c, carry):
        r0 = pl.multiple_of(c * qc, qc)
        qrows = q_ref[pl.ds(r0, qc), :]
        kvb = kvx_ref[pl.ds(r0, band), :]
        if masked:
            valid = (tile_start + r0 - WINDOW + keypos) >= 0
        pairs = range(n_pairs)

        def scores(p):
            qp = qrows[:, p * LANES:(p + 1) * LANES]
            zero = jnp.zeros_like(qp)
            lhs = jnp.concatenate([jnp.where(low, qp, zero), jnp.where(low, zero, qp)], axis=0)
            hk = p // pairs_per_kv
            s = _dot_nt(lhs, kvb[:, hk * LANES:(hk + 1) * LANES]) * scale + bias_ref[p]
            return jnp.where(valid, s, -jnp.inf) if masked else s

        s = [scores(p) for p in pairs]
        sink = [jnp.where(row2 < qc, sink_ref[2 * p], sink_ref[2 * p + 1]) for p in pairs]
        m = [jnp.maximum(jnp.max(s[p], axis=-1, keepdims=True), sink[p]) for p in pairs]
        e = [jnp.exp(s[p] - m[p]) for p in pairs]
        den = [jnp.sum(e[p], axis=-1, keepdims=True) + jnp.exp(sink[p] - m[p]) for p in pairs]
        o2 = [_dot(e[p].astype(BF16),
                   kvb[:, kvw + (p // pairs_per_kv) * LANES: kvw + (p // pairs_per_kv + 1) * LANES]) for p in pairs]
        for p in pairs:
            o = o2[p] / den[p]
            o_ref[pl.ds(r0, qc), p * LANES:(p + 1) * LANES] = jnp.where(low, o[:qc], o[qc:]).astype(BF16)
        return carry

    lax.fori_loop(0, tq // qc, chunk, 0)


def _attn(q, kv, prev, prev_map, bias, sinks, *, nseq, seq, tq, qc, masked):
    M, D = q.shape
    KV2 = kv.shape[1]
    npt = seq // tq
    row = lambda b, i: (b * npt + i, 0)
    return pl.pallas_call(
        functools.partial(_attn_kernel, qc=qc, masked=masked),
        grid=(nseq, npt),
        in_specs=[pl.BlockSpec(memory_space=pltpu.SMEM),
                  pl.BlockSpec((tq, D), row),
                  pl.BlockSpec((WINDOW, KV2), prev_map),
                  pl.BlockSpec((tq, KV2), row),
                  _resident(bias.shape)],
        out_specs=pl.BlockSpec((tq, D), row),
        out_shape=jax.ShapeDtypeStruct((M, D), BF16),
        scratch_shapes=[pltpu.VMEM((WINDOW + tq, KV2), BF16)],
        compiler_params=_cparams("parallel", "parallel"),
        name="attn_core",
    )(sinks, q, prev, kv, bias)


def _rel_bucket(rp):
    nb = NUM_BUCKETS // 2
    ret = (rp > 0).astype(jnp.int32) * nb
    n = jnp.abs(rp)
    max_exact = nb // 2
    nf = jnp.maximum(n, 1).astype(F32)
    large = max_exact + (jnp.log(nf / max_exact) / math.log(MAX_DISTANCE / max_exact)
                         * (nb - max_exact)).astype(jnp.int32)
    large = jnp.minimum(large, nb - 1)
    return ret + jnp.where(n < max_exact, n, large)


def _pair_bias(table, q_len, k_len, k_offset):
    i = jnp.arange(q_len)[:, None]
    j = jnp.arange(k_len)[None, :]
    b = table[_rel_bucket(j - k_offset - i)]
    return jnp.transpose(b, (2, 0, 1)).reshape(table.shape[1] // 2, 2 * q_len, k_len)


def _dup_heads(t, n_kv):
    lead = t.shape[:-1]
    t = t.reshape(lead + (n_kv, 1, HEAD_DIM))
    return jnp.broadcast_to(t, lead + (n_kv, 2, HEAD_DIM)).reshape(lead + (n_kv * LANES,))


def _conv_kernel(x_ref, g_ref, w_ref, cw_ref, u0_ref, z_ref, un_ref, carry_ref):
    @pl.when(pl.program_id(1) == 0)
    def _():
        carry_ref[...] = u0_ref[0]

    D = x_ref.shape[1]
    h = _rms(x_ref[...], g_ref[0:1, :]).astype(BF16)
    p = _dot(h, w_ref[...])
    bg, u = p[:, :D], p[:, D:2 * D] * p[:, 2 * D:]
    tm = u.shape[0]
    row = lax.broadcasted_iota(jnp.int32, (tm, 1), 0)
    c0, c1 = carry_ref[0:1, :], carry_ref[1:2, :]
    um1 = jnp.where(row == 0, c1, pltpu.roll(u, 1, axis=0))
    um2 = jnp.where(row == 0, c0, jnp.where(row == 1, c1, pltpu.roll(u, 2, axis=0)))
    cw = cw_ref[...]
    y = um2 * cw[0:1] + um1 * cw[1:2] + u * cw[2:3]
    z_ref[...] = (bg * y).astype(BF16)
    carry_ref[...] = u[tm - 2:, :]
    un_ref[0] = u[tm - 2:, :]


def _conv(x, g4, w_in, cw, u0, *, nseq, seq, tm):
    M, D = x.shape
    npt = seq // tm
    row = lambda b, i: (b * npt + i, 0)
    st = lambda b, i: (b, 0, 0)
    return pl.pallas_call(
        _conv_kernel,
        grid=(nseq, npt),
        in_specs=[pl.BlockSpec((tm, D), row), _resident((4, D)), _resident(w_in.shape),
                  _resident(cw.shape), pl.BlockSpec((1, CONV_WIDTH - 1, D), st)],
        out_specs=[pl.BlockSpec((tm, D), row), pl.BlockSpec((1, CONV_WIDTH - 1, D), st)],
        out_shape=[jax.ShapeDtypeStruct((M, D), BF16),
                   jax.ShapeDtypeStruct((nseq, CONV_WIDTH - 1, D), F32)],
        scratch_shapes=[pltpu.VMEM((CONV_WIDTH - 1, D), F32)],
        compiler_params=_cparams("arbitrary", "arbitrary"),
        name="conv_mix",
    )(x, g4, w_in, cw, u0)


def _rwkv_proj_kernel(x_ref, g_ref, mu_ref, wrkv_ref, vec_ref, w1_ref, w2_ref, a1_ref, a2_ref,
                      g1_ref, g2_ref, sh0_ref,
                      r_ref, k_ref, v_ref, lw_ref, a_ref, gate_ref, sh_ref, carry_ref):
    @pl.when(pl.program_id(1) == 0)
    def _():
        carry_ref[...] = sh0_ref[0]

    h = _rms(x_ref[...], g_ref[0:1, :])
    tm = h.shape[0]
    row = lax.broadcasted_iota(jnp.int32, (tm, 1), 0)
    xx = jnp.where(row == 0, carry_ref[...], pltpu.roll(h, 1, axis=0)) - h
    mu = mu_ref[...]
    mix = lambda i: (h + xx * mu[i:i + 1]).astype(BF16)
    r_ref[...] = _dot(mix(0), wrkv_ref[0]).astype(BF16)
    k_ref[...] = _dot(mix(2), wrkv_ref[1])
    v_ref[...] = _dot(mix(3), wrkv_ref[2]).astype(BF16)
    vec = vec_ref[...]
    wl = vec[0:1] + _dot(jnp.tanh(_dot(mix(1), w1_ref[...])).astype(BF16), w2_ref[...])
    t = -wl
    sp = jnp.maximum(t, 0.0) + jnp.log1p(jnp.exp(-jnp.abs(t)))
    lw_ref[...] = -jnp.exp(-sp - 0.5)
    a_ref[...] = _sigmoid(vec[1:2] + _dot(_dot(mix(4), a1_ref[...]).astype(BF16), a2_ref[...]))
    gate_ref[...] = _dot(_sigmoid(_dot(mix(5), g1_ref[...])).astype(BF16), g2_ref[...]).astype(BF16)
    carry_ref[...] = h[tm - 1:, :]
    sh_ref[0] = h[tm - 1:, :]


def _rwkv_proj(x, g4, mu, wrkv, vec, w1, w2, a1, a2, g1, g2, sh0, *, nseq, seq, tm):
    M, D = x.shape
    npt = seq // tm
    row = lambda b, i: (b * npt + i, 0)
    st = lambda b, i: (b, 0, 0)
    big = pl.BlockSpec((tm, D), row)
    sd = lambda dt: jax.ShapeDtypeStruct((M, D), dt)
    return pl.pallas_call(
        _rwkv_proj_kernel,
        grid=(nseq, npt),
        in_specs=[big, _resident((4, D)), _resident(mu.shape), _resident(wrkv.shape), _resident(vec.shape),
                  _resident(w1.shape), _resident(w2.shape), _resident(a1.shape), _resident(a2.shape),
                  _resident(g1.shape), _resident(g2.shape), pl.BlockSpec((1, 1, D), st)],
        out_specs=[big, big, big, big, big, big, pl.BlockSpec((1, 1, D), st)],
        out_shape=[sd(BF16), sd(F32), sd(BF16), sd(F32), sd(F32), sd(BF16),
                   jax.ShapeDtypeStruct((nseq, 1, D), F32)],
        scratch_shapes=[pltpu.VMEM((1, D), F32)],
        compiler_params=_cparams("arbitrary", "arbitrary"),
        name="rwkv_proj",
    )(x, g4, mu, wrkv, vec, w1, w2, a1, a2, g1, g2, sh0)


def _split_rows(x):
    hi = x.astype(BF16)
    lo = (x - hi.astype(F32)).astype(BF16)
    return jnp.concatenate([hi, lo], axis=0)


def _wkv_chunk(r, k, v, lw, a, gate, S, prm):
    C = r[0].shape[0]
    C2 = 2 * C
    H = RWKV_HEAD
    pm = lambda f, *ls: [f(*xs) for xs in zip(*ls)]
    bf = lambda xs: [x.astype(BF16) for x in xs]
    lane = lax.broadcasted_iota(jnp.int32, (1, LANES), 1)
    low = lane < H
    hr = lax.broadcasted_iota(jnp.int32, (LANES, LANES), 0)
    hc = lax.broadcasted_iota(jnp.int32, (LANES, LANES), 1)
    head_bd = (hr < H) == (hc < H)
    ones_bd = jnp.where(head_bd, 1.0, 0.0).astype(BF16)
    fold = lambda x: x[:C] + x[C:]

    def segsum(xs):
        return [fold(s) for s in [_dot(x2, ones_bd) for x2 in [_split_rows(x) for x in xs]]]

    def stack(x):
        z = jnp.zeros_like(x)
        return jnp.concatenate([jnp.where(low, x, z), jnp.where(low, z, x)], axis=0)

    tr = lax.broadcasted_iota(jnp.int32, (C, C), 0)
    tc = lax.broadcasted_iota(jnp.int32, (C, C), 1)
    tri = jnp.where(tr >= tc, 1.0, 0.0).astype(BF16)

    def hilo_cols(x):
        hi = x.astype(BF16)
        return jnp.concatenate([hi, (x - hi.astype(F32)).astype(BF16)], axis=1)

    cum2 = [_dot(tri, x) for x in pm(hilo_cols, lw)]
    cum = [c2[:, :LANES] + c2[:, LANES:] for c2 in cum2]
    G = [jnp.exp(c) for c in cum]
    Gm1 = pm(lambda c, l: jnp.exp(c - l), cum, lw)
    iG = [jnp.exp(-c) for c in cum]
    G_end = [jnp.exp(c[C - 1:C, :]) for c in cum]
    G_rest = [jnp.exp(c[C - 1:C, :] - c) for c in cum]

    kkr = pm(lambda k_, p_: k_ * p_[0:1], k, prm)
    ss = segsum([x * x for x in kkr])
    kk = pm(lambda x, s_: x * lax.rsqrt(jnp.maximum(s_, 1e-24)), kkr, ss)
    k2 = pm(lambda k_, a_, p_: k_ * (1.0 + (a_ - 1.0) * p_[1:2]), k, a, prm)
    b = pm(lambda x, a_: x * a_, kk, a)
    At = pm(lambda x, g_: -x * g_, kk, Gm1)
    Rt = pm(lambda x, g_: x * g_, r, G)

    rr = lax.broadcasted_iota(jnp.int32, (C2, C2), 0)
    cc = lax.broadcasted_iota(jnp.int32, (C2, C2), 1)
    same = (rr >= C) == (cc >= C)
    strict = same & (rr > cc)
    incl = same & (rr >= cc)

    lhs = pm(lambda x, y: jnp.concatenate([stack(x), stack(y)], axis=0).astype(BF16), At, Rt)
    Bt = bf(pm(lambda x, g_: x * g_, b, iG))
    Kt = bf(pm(lambda x, g_: x * g_, k2, iG))
    P = pm(lambda l_, b_, k_: _dot_nt(l_, jnp.concatenate([b_, b_, k_, k_], axis=0)), lhs, Bt, Kt)
    Lab = [jnp.where(strict, x[:C2, :C2], 0.0) for x in P]
    Lak = [fold(jnp.where(strict, x[:C2, C2:], 0.0)) for x in P]
    Lrr = [jnp.concatenate([fold(jnp.where(incl, x[C2:, :C2], 0.0)), fold(jnp.where(incl, x[C2:, C2:], 0.0))],
                           axis=1).astype(BF16) for x in P]

    blk = lambda s: (rr >> int(math.log2(s))) == (cc >> int(math.log2(s)))
    eye = jnp.where(rr == cc, 1.0, 0.0)
    L1 = [jnp.where(blk(8), x, 0.0) for x in Lab]
    L1b = bf(L1)
    L2 = [_dot(x, x) for x in L1b]
    L2b = bf(L2)
    L4 = [_dot(x, x) for x in L2b]
    T = pm(lambda x, y: _dot((eye + x).astype(BF16), (eye + y).astype(BF16)), L1, L2)
    T = pm(lambda x, y: _dot(x.astype(BF16), (eye + y).astype(BF16)), T, L4)
    s = 8
    while s < C:
        msk = blk(2 * s) & jnp.logical_not(blk(s))
        Mx = [jnp.where(msk, x, 0.0).astype(BF16) for x in Lab]
        Tb = bf(T)
        TM = bf(pm(_dot, Tb, Mx))
        T = pm(lambda t_, tm_, tb_: t_ + _dot(tm_, tb_), T, TM, Tb)
        s *= 2
    Tb = bf(T)

    AZ = pm(lambda x, y, s_: _dot_nt(jnp.concatenate([x, y], axis=0).astype(BF16), s_.astype(BF16)), At, Rt, S)
    Vst = [stack(x).astype(BF16) for x in v]
    rhs = pm(lambda az, l_, v_: az[:C] + _dot(l_.astype(BF16), v_), AZ, Lak, Vst)
    Ust = pm(lambda t_, x: _dot(t_, stack(x).astype(BF16)), Tb, rhs)
    Y = pm(lambda az, l_, u_, v_: az[C:] + _dot(l_, jnp.concatenate([u_.astype(BF16), v_], axis=0)),
           AZ, Lrr, Ust, Vst)
    upd = pm(lambda u_, v_, b_, k_, g_: _dot_tn(jnp.concatenate([fold(u_), v_], axis=0).astype(BF16),
                                                jnp.concatenate([b_ * g_, k_ * g_], axis=0).astype(BF16)),
             Ust, v, b, k2, G_rest)
    S_new = pm(lambda s_, g_, u_: s_ * g_ + jnp.where(head_bd, u_, 0.0), S, G_end, upd)

    mean = [x * (1.0 / H) for x in segsum(Y)]
    d = pm(lambda y_, m_: y_ - m_, Y, mean)
    var = [x * (1.0 / H) for x in segsum([x * x for x in d])]
    rk = segsum(pm(lambda r_, k_, p_: r_ * k_ * p_[2:3], r, k2, prm))
    out = pm(lambda d_, var_, p_, rk_, v_, g_: (d_ * lax.rsqrt(var_ + GN_EPS) * p_[3:4] + p_[4:5] + rk_ * v_) * g_,
             d, var, prm, rk, v, gate)
    return out, S_new


def _wkv_kernel(r_ref, k_ref, v_ref, lw_ref, a_ref, g_ref, prm_ref, s0_ref, z_ref, sf_ref, *, C):
    @pl.when(pl.program_id(1) == 0)
    def _():
        sf_ref[...] = s0_ref[...]

    tb, D = r_ref.shape
    n_pairs = D // LANES
    lanes = [slice(p * LANES, (p + 1) * LANES) for p in range(n_pairs)]

    def chunk(c, carry):
        rows = pl.ds(pl.multiple_of(c * C, C), C)
        ld = lambda ref: [ref[rows, ln].astype(F32) for ln in lanes]
        z, s_new = _wkv_chunk(ld(r_ref), ld(k_ref), ld(v_ref), ld(lw_ref), ld(a_ref), ld(g_ref),
                              [sf_ref[0, p] for p in range(n_pairs)], [prm_ref[:, ln] for ln in lanes])
        for p, ln in enumerate(lanes):
            z_ref[rows, ln] = z[p].astype(BF16)
            sf_ref[0, p] = s_new[p]
        return carry

    lax.fori_loop(0, tb // C, chunk, 0)


def _wkv(r, k, v, lw, a, gate, prm, s0, *, nseq, seq, tb, C):
    M, D = r.shape
    npt = seq // tb
    row = lambda b, i: (b * npt + i, 0)
    big = pl.BlockSpec((tb, D), row)
    st = pl.BlockSpec((1,) + s0.shape[1:], lambda b, i: (b, 0, 0, 0))
    return pl.pallas_call(
        functools.partial(_wkv_kernel, C=C),
        grid=(nseq, npt),
        in_specs=[big, big, big, big, big, big, _resident(prm.shape), st],
        out_specs=[big, st],
        out_shape=[jax.ShapeDtypeStruct((M, D), BF16), jax.ShapeDtypeStruct(s0.shape, F32)],
        compiler_params=_cparams("arbitrary", "arbitrary"),
        name="rwkv_wkv",
    )(r, k, v, lw, a, gate, prm, s0)


def _pair_state(s):
    B, Hh, N, _ = s.shape
    s = s.reshape(B, Hh // 2, 2, N, N)
    z = jnp.zeros((B, Hh // 2, 2, N, 2, N), s.dtype)
    z = z.at[:, :, 0, :, 0, :].set(s[:, :, 0]).at[:, :, 1, :, 1, :].set(s[:, :, 1])
    return z.reshape(B, Hh // 2, 2 * N, 2 * N)


def _unpair_state(z):
    B, Hp, N2, _ = z.shape
    N = N2 // 2
    z = z.reshape(B, Hp, 2, N, 2, N)
    return jnp.stack([z[:, :, 0, :, 0, :], z[:, :, 1, :, 1, :]], axis=2).reshape(B, 2 * Hp, N, N)


def _tile(n, pref):
    t = min(n, pref)
    assert n % t == 0
    return t


def _trunk(x3, prompt, a_k, a_v, b_wkv, b_shift, c_conv, W):
    B, T, D = x3.shape
    M = B * T
    x = x3.reshape(M, D)
    depth = W['norm_g'].shape[0]
    n_heads = D // HEAD_DIM
    n_kv = a_k.shape[3] if a_k is not None else W['n_kv']
    nk, nv, nwkv, nsh, ncv = [], [], [], [], []
    seq_tile = _tile(T, 512)
    for i in range(depth):
        kind, j = i % N_MIXERS, i // N_MIXERS
        g4 = W['norm_g'][i]
        if kind == 0:
            q, kv, tail = _qkv(x, g4, W['a_w_qkv'][j], seq=T, tm=seq_tile)
            kvw = n_kv * LANES
            tail_k = tail[:, :, :kvw].reshape(B, -1, n_kv, 2, HEAD_DIM)[:, :, :, 0]
            tail_v = tail[:, :, kvw:].reshape(B, -1, n_kv, 2, HEAD_DIM)[:, :, :, 0]
            if prompt:
                qc = CHUNK
                npt128 = T // WINDOW
                step128 = seq_tile // WINDOW
                prev = kv
                prev_map = lambda b, t, n=npt128, s=step128: (jnp.maximum(b * n + t * s - 1, 0), 0)
                k_new, v_new = tail_k, tail_v
            else:
                qc = T
                ck, cv = a_k[j], a_v[j]
                prev = jnp.concatenate([_dup_heads(ck.reshape(B, WINDOW, -1), n_kv),
                                        _dup_heads(cv.reshape(B, WINDOW, -1), n_kv)],
                                       axis=-1).astype(BF16).reshape(B * WINDOW, 2 * kvw)
                prev_map = lambda b, t: (b, 0)
                k_new = jnp.concatenate([ck, tail_k], axis=1)[:, -WINDOW:]
                v_new = jnp.concatenate([cv, tail_v], axis=1)[:, -WINDOW:]
            bias = _pair_bias(W['rel_bias_table'], qc, WINDOW + qc, WINDOW)
            z = _attn(q, kv, prev, prev_map, bias, W['a_sinks'][j], nseq=B, seq=T, tq=seq_tile, qc=qc,
                      masked=prompt)
            nk.append(k_new)
            nv.append(v_new)
            w_out = W['a_w_o'][j]
        elif kind == 1:
            if prompt:
                sh0 = jnp.zeros((B, 1, D), F32)
                s0 = jnp.zeros((B, n_heads // 2, LANES, LANES), F32)
            else:
                sh0 = b_shift[j].reshape(B, 1, D)
                s0 = _pair_state(b_wkv[j])
            r, k, v, lw, a, gate, sh_new = _rwkv_proj(
                x, g4, W['b_mu'][j], W['b_w_rkv'][j], W['b_vec'][j], W['b_w1'][j], W['b_w2'][j],
                W['b_a1'][j], W['b_a2'][j], W['b_g1'][j], W['b_g2'][j], sh0,
                nseq=B, seq=T, tm=_tile(T, 256))
            z, s_fin = _wkv(r, k, v, lw, a, gate, W['b_prm'][j], s0, nseq=B, seq=T,
                            tb=_tile(T, 256), C=min(T, CHUNK))
            nwkv.append(_unpair_state(s_fin))
            nsh.append(sh_new.reshape(B, D))
            w_out = W['b_w_o'][j]
        else:
            u0 = jnp.zeros((B, CONV_WIDTH - 1, D), F32) if prompt else c_conv[j]
            z, u_new = _conv(x, g4, W['c_w_in'][j], W['c_conv_w'][j], u0, nseq=B, seq=T, tm=seq_tile)
            ncv.append(u_new)
            w_out = W['c_w_out'][j]
        x = _post(x, z, w_out, g4, W['mlp_w1'][i], W['mlp_w2'][i], tm=_tile(M, 512))
    return (x.reshape(B, T, D), jnp.stack(nk), jnp.stack(nv), jnp.stack(nwkv), jnp.stack(nsh),
            jnp.stack(ncv))


def kernel(x_prompt, x_sample, cache_a_k, cache_a_v, state_b_wkv, state_b_shift, state_c_conv, rel_bias_table, norm_g, a_w_qkv, a_w_o, a_sinks, b_mu, b_w_rkv, b_w_o, b_w0, b_w1, b_w2, b_a0, b_a1, b_a2, b_g1, b_g2, b_k_k, b_k_a, b_r_k, b_ln_w, b_ln_b, c_w_in, c_conv_w, c_w_out, mlp_w1, mlp_w2):
    D = x_prompt.shape[-1]
    n_kv = cache_a_k.shape[3]
    nq = D
    nkv = n_kv * HEAD_DIM
    bf = lambda t: t.astype(BF16)
    wq, wk, wv = a_w_qkv[..., :nq], a_w_qkv[..., nq:nq + nkv], a_w_qkv[..., nq + nkv:]
    w_qkv = bf(jnp.concatenate([wq, _dup_heads(wk, n_kv), _dup_heads(wv, n_kv)], axis=-1))
    n_b = b_mu.shape[0]
    zeros = jnp.zeros_like(b_w0)
    W = dict(
        n_kv=n_kv, rel_bias_table=rel_bias_table, norm_g=norm_g, a_w_qkv=w_qkv, a_w_o=bf(a_w_o), a_sinks=a_sinks,
        b_mu=b_mu, b_w_rkv=bf(b_w_rkv), b_w_o=bf(b_w_o),
        b_vec=jnp.stack([b_w0, b_a0], axis=1),
        b_w1=bf(b_w1), b_w2=bf(b_w2), b_a1=bf(b_a1), b_a2=bf(b_a2), b_g1=bf(b_g1), b_g2=bf(b_g2),
        b_prm=jnp.stack([b_k_k, b_k_a, b_r_k.reshape(n_b, D), b_ln_w, b_ln_b, zeros, zeros, zeros], axis=1),
        c_w_in=bf(c_w_in), c_conv_w=c_conv_w, c_w_out=bf(c_w_out), mlp_w1=bf(mlp_w1), mlp_w2=bf(mlp_w2))
    y_p, ak_p, av_p, wkv_p, sh_p, cv_p = _trunk(x_prompt, True, None, None, None, None, None, W)
    y_s, ak_s, av_s, wkv_s, sh_s, cv_s = _trunk(x_sample, False, cache_a_k, cache_a_v, state_b_wkv,
                                                state_b_shift, state_c_conv, W)
    return (y_p, y_s, ak_p, av_p, ak_s, av_s, wkv_p, wkv_s, sh_p, sh_s, cv_p, cv_s)
```

```python
import functools
import math

import jax
import jax.numpy as jnp
from jax import lax
from jax.experimental import pallas as pl
from jax.experimental.pallas import tpu as pltpu

F32 = jnp.float32
BF16 = jnp.bfloat16

HEAD_DIM = 64
CHUNK = 64
WINDOW = 128
NUM_BUCKETS = 32
MAX_DISTANCE = 128
RWKV_HEAD = 64
CONV_WIDTH = 3
N_MIXERS = 3
RMS_EPS = 1e-6
GN_EPS = RWKV_HEAD * 1e-5

LANES = 128
VMEM_LIMIT = 56 * 1024 * 1024


def _cparams(*sem):
    return pltpu.CompilerParams(dimension_semantics=sem, vmem_limit_bytes=VMEM_LIMIT)


def _resident(arr, layer=None):
    if layer is None:
        nd = arr.ndim
        return pl.BlockSpec(arr.shape, lambda *_: (0,) * nd, pipeline_mode=pl.Buffered(1))
    nd = arr.ndim - 1
    return pl.BlockSpec((None,) + arr.shape[1:], lambda *_: (layer,) + (0,) * nd, pipeline_mode=pl.Buffered(1))


def _rms(x, g):
    return x * lax.rsqrt(jnp.mean(x * x, axis=-1, keepdims=True) + RMS_EPS) * g


def _dot(a, b):
    return jnp.dot(a, b, preferred_element_type=F32)


def _dot_nt(a, b):
    return lax.dot_general(a, b, (((1,), (1,)), ((), ())), preferred_element_type=F32)


def _dot_tn(a, b):
    return lax.dot_general(a, b, (((0,), (0,)), ((), ())), preferred_element_type=F32)


def _sigmoid(x):
    return 1.0 / (1.0 + jnp.exp(-x))


def _post_kernel(x_ref, z_ref, wo_ref, g_ref, w1_ref, w2_ref, o_ref, acc_ref, *, ff_chunk):
    g = g_ref[...]
    m = _dot(z_ref[...], wo_ref[...])
    x1 = x_ref[...] + _rms(m, g[1:2])
    h2 = _rms(x1, g[2:3]).astype(BF16)
    d_ff = w1_ref.shape[1]
    for c in range(d_ff // ff_chunk):
        sl = slice(c * ff_chunk, (c + 1) * ff_chunk)
        a = jnp.maximum(_dot(h2, w1_ref[:, sl]), 0.0)
        part = _dot((a * a).astype(BF16), w2_ref[sl, :])
        if c == 0:
            acc_ref[...] = part
        else:
            acc_ref[...] += part
    o_ref[...] = x1 + _rms(acc_ref[...], g[3:4])


def _post(x, z, w_out, j, norm_g, w1, w2, i, *, tm):
    M, D = x.shape
    d_ff = w1.shape[2]
    row = lambda t: (t, 0)
    return pl.pallas_call(
        functools.partial(_post_kernel, ff_chunk=min(d_ff, 1024)),
        grid=(M // tm,),
        in_specs=[pl.BlockSpec((tm, D), row), pl.BlockSpec((tm, D), row),
                  _resident(w_out, j), _resident(norm_g, i), _resident(w1, i), _resident(w2, i)],
        out_specs=pl.BlockSpec((tm, D), row),
        out_shape=jax.ShapeDtypeStruct((M, D), F32),
        scratch_shapes=[pltpu.VMEM((tm, D), F32)],
        compiler_params=_cparams("parallel"),
        name="post_mlp",
    )(x, z, w_out, norm_g, w1, w2)


def _qkv_kernel(x_ref, g_ref, w_ref, q_ref, kv_ref, tail_ref, *, tiles_per_seq, tail):
    h = _rms(x_ref[...], g_ref[0:1, :]).astype(BF16)
    p = _dot(h, w_ref[...])
    D = q_ref.shape[1]
    q_ref[...] = p[:, :D].astype(BF16)
    kv_ref[...] = p[:, D:].astype(BF16)
    tm = p.shape[0]

    @pl.when(pl.program_id(0) % tiles_per_seq == tiles_per_seq - 1)
    def _():
        tail_ref[0] = p[tm - tail:, D:]


def _qkv(x, norm_g, i, w, j, *, seq, tm):
    M, D = x.shape
    KV = w.shape[2] - D
    tail = min(WINDOW, seq)
    assert seq % tm == 0 and tail <= tm
    tps = seq // tm
    row = lambda i: (i, 0)
    return pl.pallas_call(
        functools.partial(_qkv_kernel, tiles_per_seq=tps, tail=tail),
        grid=(M // tm,),
        in_specs=[pl.BlockSpec((tm, D), row), _resident(norm_g, i), _resident(w, j)],
        out_specs=[pl.BlockSpec((tm, D), row), pl.BlockSpec((tm, KV), row),
                   pl.BlockSpec((1, tail, KV), lambda t: (t // tps, 0, 0))],
        out_shape=[jax.ShapeDtypeStruct((M, D), BF16), jax.ShapeDtypeStruct((M, KV), BF16),
                   jax.ShapeDtypeStruct((M // seq, tail, KV), F32)],
        compiler_params=_cparams("arbitrary"),
        name="attn_qkv",
    )(x, norm_g, w)


def _attn_kernel(sink_ref, table_ref, bucket_ref, q_ref, prev_ref, cur_ref, o_ref, kvx_ref, bias_ref, *,
                 qc, masked):
    tq, D = q_ref.shape
    kvw = cur_ref.shape[1] // 2
    band = WINDOW + qc
    n_var = bias_ref.shape[0]
    n_pairs = D // LANES
    pairs_per_kv = n_pairs // (kvw // LANES)
    keypos = lax.broadcasted_iota(jnp.int32, (1, band), 1)

    @pl.when((pl.program_id(0) == 0) & (pl.program_id(1) == 0))
    def _():
        bucket = bucket_ref[...]
        for h in range(2 * n_pairs):
            pick = lambda b, acc: jnp.where(bucket == b, table_ref[b, h], acc)
            bias_h = lax.fori_loop(0, table_ref.shape[0], pick, jnp.zeros((qc, band), F32))
            for v in range(n_var):
                hidden = keypos < (WINDOW - v * qc if masked else 0)
                bias_ref[v, h // 2, (h % 2) * qc:(h % 2 + 1) * qc, :] = jnp.where(hidden, -jnp.inf, bias_h)

    kvx_ref[0:WINDOW, :] = prev_ref[...]
    kvx_ref[WINDOW:, :] = cur_ref[...]
    lane = lax.broadcasted_iota(jnp.int32, (1, LANES), 1)
    low = lane < HEAD_DIM
    row2 = lax.broadcasted_iota(jnp.int32, (2 * qc, 1), 0)
    first_chunk = pl.program_id(1) * (tq // qc)

    def chunk(c, carry):
        r0 = pl.multiple_of(c * qc, qc)
        qrows = q_ref[pl.ds(r0, qc), :]
        kvb = kvx_ref[pl.ds(r0, band), :]
        var = jnp.minimum(first_chunk + c, n_var - 1)
        pairs = range(n_pairs)

        def scores(p):
            qp = qrows[:, p * LANES:(p + 1) * LANES]
            zero = jnp.zeros_like(qp)
            lhs = jnp.concatenate([jnp.where(low, qp, zero), jnp.where(low, zero, qp)], axis=0)
            hk = p // pairs_per_kv
            return _dot_nt(lhs, kvb[:, hk * LANES:(hk + 1) * LANES]) + bias_ref[var, p]

        s = [scores(p) for p in pairs]
        sink = [jnp.where(row2 < qc, sink_ref[2 * p], sink_ref[2 * p + 1]) for p in pairs]
        m = [jnp.maximum(jnp.max(s[p], axis=-1, keepdims=True), sink[p]) for p in pairs]
        e = [jnp.exp(s[p] - m[p]) for p in pairs]
        den = [jnp.sum(e[p], axis=-1, keepdims=True) + jnp.exp(sink[p] - m[p]) for p in pairs]
        o2 = [_dot(e[p].astype(BF16),
                   kvb[:, kvw + (p // pairs_per_kv) * LANES: kvw + (p // pairs_per_kv + 1) * LANES]) for p in pairs]
        for p in pairs:
            o = o2[p] / den[p]
            o_ref[pl.ds(r0, qc), p * LANES:(p + 1) * LANES] = jnp.where(low, o[:qc], o[qc:]).astype(BF16)
        return carry

    lax.fori_loop(0, tq // qc, chunk, 0)


def _attn(q, kv, prev, prev_map, table, sinks, *, nseq, seq, tq, qc, masked):
    M, D = q.shape
    KV2 = kv.shape[1]
    npt = seq // tq
    band = WINDOW + qc
    n_var = WINDOW // qc + 1 if masked else 1
    qi = jnp.arange(qc)[:, None]
    kj = jnp.arange(band)[None, :]
    bucket = _rel_bucket(kj - WINDOW - qi)
    row = lambda b, t: (b * npt + t, 0)
    smem = pl.BlockSpec(memory_space=pltpu.SMEM)
    return pl.pallas_call(
        functools.partial(_attn_kernel, qc=qc, masked=masked),
        grid=(nseq, npt),
        in_specs=[smem, smem, _resident(bucket),
                  pl.BlockSpec((tq, D), row),
                  pl.BlockSpec((WINDOW, KV2), prev_map),
                  pl.BlockSpec((tq, KV2), row)],
        out_specs=pl.BlockSpec((tq, D), row),
        out_shape=jax.ShapeDtypeStruct((M, D), BF16),
        scratch_shapes=[pltpu.VMEM((WINDOW + tq, KV2), BF16),
                        pltpu.VMEM((n_var, D // LANES, 2 * qc, band), F32)],
        compiler_params=_cparams("arbitrary", "arbitrary"),
        name="attn_core",
    )(sinks, table, bucket, q, prev, kv)


def _rel_bucket(rp):
    nb = NUM_BUCKETS // 2
    ret = (rp > 0).astype(jnp.int32) * nb
    n = jnp.abs(rp)
    max_exact = nb // 2
    nf = jnp.maximum(n, 1).astype(F32)
    large = max_exact + (jnp.log(nf / max_exact) / math.log(MAX_DISTANCE / max_exact)
                         * (nb - max_exact)).astype(jnp.int32)
    large = jnp.minimum(large, nb - 1)
    return ret + jnp.where(n < max_exact, n, large)


def _dup_heads(t, n_kv):
    lead = t.shape[:-1]
    t = t.reshape(lead + (n_kv, 1, HEAD_DIM))
    return jnp.broadcast_to(t, lead + (n_kv, 2, HEAD_DIM)).reshape(lead + (n_kv * LANES,))


def _conv_kernel(x_ref, g_ref, w_ref, cw_ref, u0_ref, z_ref, un_ref, carry_ref):
    @pl.when(pl.program_id(1) == 0)
    def _():
        carry_ref[...] = u0_ref[0]

    D = x_ref.shape[1]
    h = _rms(x_ref[...], g_ref[0:1, :]).astype(BF16)
    p = _dot(h, w_ref[...])
    bg, u = p[:, :D], p[:, D:2 * D] * p[:, 2 * D:]
    tm = u.shape[0]
    row = lax.broadcasted_iota(jnp.int32, (tm, 1), 0)
    c0, c1 = carry_ref[0:1, :], carry_ref[1:2, :]
    um1 = jnp.where(row == 0, c1, pltpu.roll(u, 1, axis=0))
    um2 = jnp.where(row == 0, c0, jnp.where(row == 1, c1, pltpu.roll(u, 2, axis=0)))
    cw = cw_ref[...]
    y = um2 * cw[0:1] + um1 * cw[1:2] + u * cw[2:3]
    z_ref[...] = (bg * y).astype(BF16)
    carry_ref[...] = u[tm - 2:, :]
    un_ref[0] = u[tm - 2:, :]


def _conv(x, norm_g, i, w_in, cw, j, u0, *, nseq, seq, tm):
    M, D = x.shape
    npt = seq // tm
    row = lambda b, t: (b * npt + t, 0)
    st = lambda b, t: (b, 0, 0)
    return pl.pallas_call(
        _conv_kernel,
        grid=(nseq, npt),
        in_specs=[pl.BlockSpec((tm, D), row), _resident(norm_g, i), _resident(w_in, j),
                  _resident(cw, j), pl.BlockSpec((1, CONV_WIDTH - 1, D), st)],
        out_specs=[pl.BlockSpec((tm, D), row), pl.BlockSpec((1, CONV_WIDTH - 1, D), st)],
        out_shape=[jax.ShapeDtypeStruct((M, D), BF16),
                   jax.ShapeDtypeStruct((nseq, CONV_WIDTH - 1, D), F32)],
        scratch_shapes=[pltpu.VMEM((CONV_WIDTH - 1, D), F32)],
        compiler_params=_cparams("arbitrary", "arbitrary"),
        name="conv_mix",
    )(x, norm_g, w_in, cw, u0)


def _rwkv_proj_kernel(x_ref, g_ref, mu_ref, wrkv_ref, vec_ref, w1_ref, w2_ref, a1_ref, a2_ref,
                      g1_ref, g2_ref, sh0_ref,
                      r_ref, k_ref, v_ref, lw_ref, a_ref, gate_ref, sh_ref, carry_ref, *, sub):
    @pl.when(pl.program_id(1) == 0)
    def _():
        carry_ref[...] = sh0_ref[0]

    tm = x_ref.shape[0]
    mu = mu_ref[...]
    vec = vec_ref[...]
    row = lax.broadcasted_iota(jnp.int32, (sub, 1), 0)

    def mixes(s, prev_last):
        h = _rms(x_ref[s * sub:(s + 1) * sub, :], g_ref[0:1, :])
        xx = jnp.where(row == 0, prev_last, pltpu.roll(h, 1, axis=0)) - h
        return h[sub - 1:, :], [(h + xx * mu[i:i + 1]).astype(BF16) for i in range(6)]

    def project(s, m):
        rows = slice(s * sub, (s + 1) * sub)
        r_ref[rows, :] = _dot(m[0], wrkv_ref[0]).astype(BF16)
        k_ref[rows, :] = _dot(m[2], wrkv_ref[1])
        v_ref[rows, :] = _dot(m[3], wrkv_ref[2]).astype(BF16)
        wl = vec[0:1] + _dot(jnp.tanh(_dot(m[1], w1_ref[...])).astype(BF16), w2_ref[...])
        t = -wl
        sp = jnp.maximum(t, 0.0) + jnp.log1p(jnp.exp(-jnp.abs(t)))
        lw_ref[rows, :] = -jnp.exp(-sp - 0.5)
        a_ref[rows, :] = _sigmoid(vec[1:2] + _dot(_dot(m[4], a1_ref[...]).astype(BF16), a2_ref[...]))
        gate_ref[rows, :] = _dot(_sigmoid(_dot(m[5], g1_ref[...])).astype(BF16), g2_ref[...]).astype(BF16)

    last, m = mixes(0, carry_ref[...])
    for s in range(tm // sub):
        nxt = mixes(s + 1, last) if (s + 1) * sub < tm else None
        project(s, m)
        if nxt is not None:
            last, m = nxt
    carry_ref[...] = last
    sh_ref[0] = last


def _rwkv_proj(x, norm_g, i, W, j, sh0, *, nseq, seq, tm, sub):
    M, D = x.shape
    npt = seq // tm
    row = lambda b, t: (b * npt + t, 0)
    st = lambda b, t: (b, 0, 0)
    big = pl.BlockSpec((tm, D), row)
    sd = lambda dt: jax.ShapeDtypeStruct((M, D), dt)
    names = ('b_mu', 'b_w_rkv', 'b_vec', 'b_w1', 'b_w2', 'b_a1', 'b_a2', 'b_g1', 'b_g2')
    return pl.pallas_call(
        functools.partial(_rwkv_proj_kernel, sub=sub),
        grid=(nseq, npt),
        in_specs=[big, _resident(norm_g, i)] + [_resident(W[n], j) for n in names]
                 + [pl.BlockSpec((1, 1, D), st)],
        out_specs=[big, big, big, big, big, big, pl.BlockSpec((1, 1, D), st)],
        out_shape=[sd(BF16), sd(F32), sd(BF16), sd(F32), sd(F32), sd(BF16),
                   jax.ShapeDtypeStruct((nseq, 1, D), F32)],
        scratch_shapes=[pltpu.VMEM((1, D), F32)],
        compiler_params=_cparams("arbitrary", "arbitrary"),
        name="rwkv_proj",
    )(x, norm_g, *[W[n] for n in names], sh0)


def _wkv_chunk(r, k, v, lw, a, gate, prm):
    C = r[0].shape[0]
    C2 = 2 * C
    H = RWKV_HEAD
    pm = lambda f, *ls: [f(*xs) for xs in zip(*ls)]
    bf = lambda xs: [x.astype(BF16) for x in xs]
    lane = lax.broadcasted_iota(jnp.int32, (1, LANES), 1)
    low = lane < H
    hr = lax.broadcasted_iota(jnp.int32, (LANES, LANES), 0)
    hc = lax.broadcasted_iota(jnp.int32, (LANES, LANES), 1)
    head_bd = (hr < H) == (hc < H)
    ones_bd = jnp.where(head_bd, 1.0, 0.0).astype(BF16)
    fold = lambda x: x[:C] + x[C:]

    def segsum(xs):
        return [_dot(x.astype(BF16), ones_bd) for x in xs]

    def stack(x):
        z = jnp.zeros_like(x)
        return jnp.concatenate([jnp.where(low, x, z), jnp.where(low, z, x)], axis=0)

    def hilo_cols(x):
        hi = x.astype(BF16)
        return jnp.concatenate([hi, (x - hi.astype(F32)).astype(BF16)], axis=1)

    def prep():
        tr = lax.broadcasted_iota(jnp.int32, (C, C), 0)
        tc = lax.broadcasted_iota(jnp.int32, (C, C), 1)
        tri = jnp.where(tr >= tc, 1.0, 0.0).astype(BF16)
        cum2 = [_dot(tri, x) for x in pm(hilo_cols, lw)]
        kkr = pm(lambda k_, p_: k_ * p_[0:1], k, prm)
        ss = segsum([x * x for x in kkr])
        yield
        cum = [c2[:, :LANES] + c2[:, LANES:] for c2 in cum2]
        G = [jnp.exp(c) for c in cum]
        Gm1 = pm(lambda c, l: jnp.exp(c - l), cum, lw)
        iG = [jnp.exp(-c) for c in cum]
        G_end = [jnp.exp(c[C - 1:C, :]) for c in cum]
        G_rest = [jnp.exp(c[C - 1:C, :] - c) for c in cum]
        kk = pm(lambda x, s_: x * lax.rsqrt(jnp.maximum(s_, 1e-24)), kkr, ss)
        k2 = pm(lambda k_, a_, p_: k_ * (1.0 + (a_ - 1.0) * p_[1:2]), k, a, prm)
        b = pm(lambda x, a_: x * a_, kk, a)
        At = pm(lambda x, g_: -x * g_, kk, Gm1)
        Rt = pm(lambda x, g_: x * g_, r, G)

        rr = lax.broadcasted_iota(jnp.int32, (C2, C2), 0)
        cc = lax.broadcasted_iota(jnp.int32, (C2, C2), 1)
        same = (rr >= C) == (cc >= C)
        strict = same & (rr > cc)
        incl = same & (rr >= cc)

        lhs = pm(lambda x, y: jnp.concatenate([stack(x), stack(y)], axis=0).astype(BF16), At, Rt)
        Bt = bf(pm(lambda x, g_: x * g_, b, iG))
        Kt = bf(pm(lambda x, g_: x * g_, k2, iG))
        P = pm(lambda l_, b_, k_: _dot_nt(l_, jnp.concatenate([b_, b_, k_, k_], axis=0)), lhs, Bt, Kt)
        rk = segsum(pm(lambda r_, k_, p_: r_ * k_ * p_[2:3], r, k2, prm))
        yield
        Lab = [jnp.where(strict, x[:C2, :C2], 0.0) for x in P]
        Lak = [fold(jnp.where(strict, x[:C2, C2:], 0.0)).astype(BF16) for x in P]
        Lrr = [jnp.concatenate([fold(jnp.where(incl, x[C2:, :C2], 0.0)), fold(jnp.where(incl, x[C2:, C2:], 0.0))],
                               axis=1).astype(BF16) for x in P]
        Vst = [stack(x).astype(BF16) for x in v]
        LV = pm(_dot, Lak, Vst)

        blk = lambda s: (rr >> int(math.log2(s))) == (cc >> int(math.log2(s)))
        eye = jnp.where(rr == cc, 1.0, 0.0)
        L1 = [jnp.where(blk(8), x, 0.0) for x in Lab]
        L1b = bf(L1)
        L2 = [_dot(x, x) for x in L1b]
        yield
        L2b = bf(L2)
        L4 = [_dot(x, x) for x in L2b]
        T = pm(lambda x, y: _dot((eye + x).astype(BF16), (eye + y).astype(BF16)), L1, L2)
        yield
        T = pm(lambda x, y: _dot(x.astype(BF16), (eye + y).astype(BF16)), T, L4)
        yield
        s = 8
        while s < C:
            msk = blk(2 * s) & jnp.logical_not(blk(s))
            upper = [(o + s, o + 2 * s) for o in range(0, C2, 2 * s)]
            Mx = [jnp.where(msk, x, 0.0).astype(BF16) for x in Lab]
            Tb = bf(T)
            Tu = [jnp.concatenate([t[a_:b_] for a_, b_ in upper], axis=0).astype(BF16) for t in T]
            TM = pm(_dot, Tu, Mx)
            yield
            X = pm(_dot, bf(TM), Tb)
            yield
            T = [jnp.concatenate([piece for n, (a_, b_) in enumerate(upper)
                                  for piece in (t[a_ - s:a_], t[a_:b_] + x[n * s:(n + 1) * s])], axis=0)
                 for t, x in zip(T, X)]
            s *= 2
        lhs_s = pm(lambda x, y: jnp.concatenate([x, y], axis=0).astype(BF16), At, Rt)
        rhs_s = pm(lambda b_, k_, g_: jnp.concatenate([b_ * g_, k_ * g_], axis=0).astype(BF16), b, k2, G_rest)
        return dict(Tb=bf(T), lhs_s=lhs_s, rhs_s=rhs_s, LV=LV, Lrr=Lrr, Vst=Vst, G_end=G_end, rk=rk)

    def apply(q, S):
        AZ = pm(lambda l_, s_: _dot_nt(l_, s_.astype(BF16)), q['lhs_s'], S)
        yield
        Ust = pm(lambda t_, az, lv: _dot(t_, stack(az[:C] + lv).astype(BF16)), q['Tb'], AZ, q['LV'])
        yield
        Y = pm(lambda az, l_, u_, v_: az[C:] + _dot(l_, jnp.concatenate([u_.astype(BF16), v_], axis=0)),
               AZ, q['Lrr'], Ust, q['Vst'])
        upd = pm(lambda u_, v_, rhs_: _dot_tn(jnp.concatenate([fold(u_), v_], axis=0).astype(BF16), rhs_),
                 Ust, v, q['rhs_s'])
        yield
        S_new = pm(lambda s_, g_, u_: s_ * g_ + jnp.where(head_bd, u_, 0.0), S, q['G_end'], upd)
        mean = [x * (1.0 / H) for x in segsum(Y)]
        yield
        d = pm(lambda y_, m_: y_ - m_, Y, mean)
        var = [x * (1.0 / H) for x in segsum([x * x for x in d])]
        yield
        out = pm(lambda d_, var_, p_, rk_, v_, g_:
                 (d_ * lax.rsqrt(var_ + GN_EPS) * p_[3:4] + p_[4:5] + rk_ * v_) * g_,
                 d, var, prm, q['rk'], v, gate)
        return out, S_new

    return prep, apply


PREP_AHEAD = 2


def _step(gen):
    try:
        next(gen)
        return False, None
    except StopIteration as stop:
        return True, stop.value


def _wkv_kernel(r_ref, k_ref, v_ref, lw_ref, a_ref, g_ref, prm_ref, s0_ref, z_ref, sf_ref, *, C):
    @pl.when(pl.program_id(1) == 0)
    def _():
        sf_ref[...] = s0_ref[...]

    tb, D = r_ref.shape
    n_pairs = D // LANES
    n_chunks = tb // C
    lanes = [slice(p * LANES, (p + 1) * LANES) for p in range(n_pairs)]
    prm = [prm_ref[:, ln] for ln in lanes]

    def stages(c):
        rows = slice(c * C, (c + 1) * C)
        ld = lambda ref: [ref[rows, ln].astype(F32) for ln in lanes]
        return _wkv_chunk(ld(r_ref), ld(k_ref), ld(v_ref), ld(lw_ref), ld(a_ref), ld(g_ref), prm)

    S = [sf_ref[0, p] for p in range(n_pairs)]
    applies, ready, in_flight = {}, {}, []
    launched = done = 0
    cur = None
    while done < n_chunks:
        while len(in_flight) < PREP_AHEAD and launched < n_chunks:
            prep, applies[launched] = stages(launched)
            in_flight.append((launched, prep()))
            launched += 1
        if cur is None and done in ready:
            cur = applies.pop(done)(ready.pop(done), S)
        if cur is not None:
            finished, res = _step(cur)
            if finished:
                z, S = res
                for p, ln in enumerate(lanes):
                    z_ref[done * C:(done + 1) * C, ln] = z[p].astype(BF16)
                cur = None
                done += 1
        for c, gen in list(in_flight):
            finished, res = _step(gen)
            if finished:
                ready[c] = res
                in_flight.remove((c, gen))
    for p in range(n_pairs):
        sf_ref[0, p] = S[p]


def _wkv(r, k, v, lw, a, gate, prm, j, s0, *, nseq, seq, tb, C):
    M, D = r.shape
    npt = seq // tb
    row = lambda b, t: (b * npt + t, 0)
    big = pl.BlockSpec((tb, D), row)
    st = pl.BlockSpec((1,) + s0.shape[1:], lambda b, t: (b, 0, 0, 0))
    return pl.pallas_call(
        functools.partial(_wkv_kernel, C=C),
        grid=(nseq, npt),
        in_specs=[big, big, big, big, big, big, _resident(prm, j), st],
        out_specs=[big, st],
        out_shape=[jax.ShapeDtypeStruct((M, D), BF16), jax.ShapeDtypeStruct(s0.shape, F32)],
        compiler_params=_cparams("arbitrary", "arbitrary"),
        name="rwkv_wkv",
    )(r, k, v, lw, a, gate, prm, s0)


def _pair_state(s):
    B, Hh, N, _ = s.shape
    s = s.reshape(B, Hh // 2, 2, N, N)
    z = jnp.zeros((B, Hh // 2, N, N), s.dtype)
    return jnp.concatenate([jnp.concatenate([s[:, :, 0], z], axis=-1),
                            jnp.concatenate([z, s[:, :, 1]], axis=-1)], axis=-2)


def _unpair_state(z):
    N = z.shape[-1] // 2
    return jnp.stack([z[:, :, :N, :N], z[:, :, N:, N:]], axis=2).reshape(z.shape[0], -1, N, N)


TILE_ROWS = dict(post=512, seq=512, rwkv_proj=512, rwkv_sub=256, wkv=256)


def _tile(n, pref):
    t = min(n, pref)
    assert n % t == 0
    return t


def _trunk(x3, prompt, a_k, a_v, b_wkv, b_shift, c_conv, W):
    B, T, D = x3.shape
    M = B * T
    x = x3.reshape(M, D)
    norm_g = W['norm_g']
    n_heads = D // HEAD_DIM
    n_kv = W['n_kv']
    kvw = n_kv * LANES
    nk, nv, nwkv, nsh, ncv = [], [], [], [], []
    seq_tile = _tile(T, TILE_ROWS['seq'])
    for i in range(norm_g.shape[0]):
        kind, j = i % N_MIXERS, i // N_MIXERS
        if kind == 0:
            q, kv, tail = _qkv(x, norm_g, i, W['a_w_qkv'], j, seq=T, tm=seq_tile)
            tail_k = tail[:, :, :kvw].reshape(B, -1, n_kv, 2, HEAD_DIM)[:, :, :, 0]
            tail_v = tail[:, :, kvw:].reshape(B, -1, n_kv, 2, HEAD_DIM)[:, :, :, 0]
            if prompt:
                qc = CHUNK
                prev = kv
                prev_map = lambda b, t, n=T // WINDOW, s=seq_tile // WINDOW: (jnp.maximum(b * n + t * s - 1, 0), 0)
                k_new, v_new = tail_k, tail_v
            else:
                qc = T
                ck, cv = a_k[j], a_v[j]
                prev = jnp.concatenate([_dup_heads(ck.reshape(B, WINDOW, -1), n_kv),
                                        _dup_heads(cv.reshape(B, WINDOW, -1), n_kv)],
                                       axis=-1).astype(BF16).reshape(B * WINDOW, 2 * kvw)
                prev_map = lambda b, t: (b, 0)
                k_new = jnp.concatenate([ck, tail_k], axis=1)[:, -WINDOW:]
                v_new = jnp.concatenate([cv, tail_v], axis=1)[:, -WINDOW:]
            z = _attn(q, kv, prev, prev_map, W['rel_bias_table'], W['a_sinks'][j], nseq=B, seq=T, tq=seq_tile,
                      qc=qc, masked=prompt)
            nk.append(k_new)
            nv.append(v_new)
            w_out = W['a_w_o']
        elif kind == 1:
            if prompt:
                sh0 = jnp.zeros((B, 1, D), F32)
                s0 = jnp.zeros((B, n_heads // 2, LANES, LANES), F32)
            else:
                sh0 = b_shift[j].reshape(B, 1, D)
                s0 = _pair_state(b_wkv[j])
            tm = _tile(T, TILE_ROWS['rwkv_proj'])
            r, k, v, lw, a, gate, sh_new = _rwkv_proj(x, norm_g, i, W, j, sh0, nseq=B, seq=T, tm=tm,
                                                      sub=_tile(tm, TILE_ROWS['rwkv_sub']))
            z, s_fin = _wkv(r, k, v, lw, a, gate, W['b_prm'], j, s0, nseq=B, seq=T,
                            tb=_tile(T, TILE_ROWS['wkv']), C=min(T, CHUNK))
            nwkv.append(_unpair_state(s_fin))
            nsh.append(sh_new.reshape(B, D))
            w_out = W['b_w_o']
        else:
            u0 = jnp.zeros((B, CONV_WIDTH - 1, D), F32) if prompt else c_conv[j]
            z, u_new = _conv(x, norm_g, i, W['c_w_in'], W['c_conv_w'], j, u0, nseq=B, seq=T, tm=seq_tile)
            ncv.append(u_new)
            w_out = W['c_w_out']
        x = _post(x, z, w_out, j, norm_g, W['mlp_w1'], W['mlp_w2'], i, tm=_tile(M, TILE_ROWS['post']))
    return (x.reshape(B, T, D), jnp.stack(nk), jnp.stack(nv), jnp.stack(nwkv), jnp.stack(nsh),
            jnp.stack(ncv))


def kernel(x_prompt, x_sample, cache_a_k, cache_a_v, state_b_wkv, state_b_shift, state_c_conv, rel_bias_table, norm_g, a_w_qkv, a_w_o, a_sinks, b_mu, b_w_rkv, b_w_o, b_w0, b_w1, b_w2, b_a0, b_a1, b_a2, b_g1, b_g2, b_k_k, b_k_a, b_r_k, b_ln_w, b_ln_b, c_w_in, c_conv_w, c_w_out, mlp_w1, mlp_w2):
    D = x_prompt.shape[-1]
    n_kv = cache_a_k.shape[3]
    nq = D
    nkv = n_kv * HEAD_DIM
    bf = lambda t: t.astype(BF16)
    wq, wk, wv = a_w_qkv[..., :nq], a_w_qkv[..., nq:nq + nkv], a_w_qkv[..., nq + nkv:]
    w_qkv = bf(jnp.concatenate([wq * HEAD_DIM ** -0.5, _dup_heads(wk, n_kv), _dup_heads(wv, n_kv)], axis=-1))
    n_b = b_mu.shape[0]
    zeros = jnp.zeros_like(b_w0)
    W = dict(
        n_kv=n_kv, rel_bias_table=rel_bias_table, norm_g=norm_g, a_w_qkv=w_qkv, a_w_o=bf(a_w_o), a_sinks=a_sinks,
        b_mu=b_mu, b_w_rkv=bf(b_w_rkv), b_w_o=bf(b_w_o),
        b_vec=jnp.stack([b_w0, b_a0], axis=1),
        b_w1=bf(b_w1), b_w2=bf(b_w2), b_a1=bf(b_a1), b_a2=bf(b_a2), b_g1=bf(b_g1), b_g2=bf(b_g2),
        b_prm=jnp.stack([b_k_k, b_k_a, b_r_k.reshape(n_b, D), b_ln_w, b_ln_b, zeros, zeros, zeros], axis=1),
        c_w_in=bf(c_w_in), c_conv_w=c_conv_w, c_w_out=bf(c_w_out), mlp_w1=bf(mlp_w1), mlp_w2=bf(mlp_w2))
    y_p, ak_p, av_p, wkv_p, sh_p, cv_p = _trunk(x_prompt, True, None, None, None, None, None, W)
    y_s, ak_s, av_s, wkv_s, sh_s, cv_s = _trunk(x_sample, False, cache_a_k, cache_a_v, state_b_wkv,
                                                state_b_shift, state_c_conv, W)
    return (y_p, y_s, ak_p, av_p, ak_s, av_s, wkv_p, wkv_s, sh_p, sh_s, cv_p, cv_s)
```

```python
import functools
import math

import jax
import jax.numpy as jnp
from jax import lax
from jax.experimental import pallas as pl
from jax.experimental.pallas import tpu as pltpu

F32 = jnp.float32
BF16 = jnp.bfloat16

HEAD_DIM = 64
CHUNK = 64
WINDOW = 128
NUM_BUCKETS = 32
MAX_DISTANCE = 128
RWKV_HEAD = 64
CONV_WIDTH = 3
N_MIXERS = 3
RMS_EPS = 1e-6
GN_EPS = RWKV_HEAD * 1e-5
LOG2E = math.log2(math.e)

LANES = 128
VMEM_LIMIT = 56 * 1024 * 1024


def _cparams(*sem):
    return pltpu.CompilerParams(dimension_semantics=sem, vmem_limit_bytes=VMEM_LIMIT)


def _resident(arr, layer=None):
    if layer is None:
        nd = arr.ndim
        return pl.BlockSpec(arr.shape, lambda *_: (0,) * nd, pipeline_mode=pl.Buffered(1))
    nd = arr.ndim - 1
    return pl.BlockSpec((None,) + arr.shape[1:], lambda *_: (layer,) + (0,) * nd, pipeline_mode=pl.Buffered(1))


def _rms(x, g):
    return x * lax.rsqrt(jnp.mean(x * x, axis=-1, keepdims=True) + RMS_EPS) * g


def _dot(a, b):
    return jnp.dot(a, b, preferred_element_type=F32)


def _dot_nt(a, b):
    return lax.dot_general(a, b, (((1,), (1,)), ((), ())), preferred_element_type=F32)


def _dot_tn(a, b):
    return lax.dot_general(a, b, (((0,), (0,)), ((), ())), preferred_element_type=F32)


def _sigmoid(x):
    return 1.0 / (1.0 + jnp.exp(-x))


def _step(gen):
    try:
        next(gen)
        return False, None
    except StopIteration as stop:
        return True, stop.value


def _run(gen):
    while not _step(gen)[0]:
        pass


def _post_stages(x_ref, load_z, wo_ref, g_ref, w1_ref, w2_ref, o_ref, acc_ref, *, ff_chunk):
    g = g_ref[...]
    m = _dot(load_z(), wo_ref[...])
    x1 = x_ref[...] + _rms(m, g[1:2])
    h2 = _rms(x1, g[2:3]).astype(BF16)
    yield
    d_ff = w1_ref.shape[1]
    for c in range(d_ff // ff_chunk):
        sl = slice(c * ff_chunk, (c + 1) * ff_chunk)
        a = jnp.maximum(_dot(h2, w1_ref[:, sl]), 0.0)
        a = (a * a).astype(BF16)
        yield
        part = _dot(a, w2_ref[sl, :])
        if c == 0:
            acc_ref[...] = part
        else:
            acc_ref[...] += part
        yield
    o_ref[...] = x1 + _rms(acc_ref[...], g[3:4])


def _post_kernel(x_ref, z_ref, wo_ref, g_ref, w1_ref, w2_ref, o_ref, acc_ref, *, ff_chunk):
    _run(_post_stages(x_ref, lambda: z_ref[...], wo_ref, g_ref, w1_ref, w2_ref, o_ref, acc_ref,
                      ff_chunk=ff_chunk))


def _post(x, z, w_out, j, norm_g, w1, w2, i, *, tm):
    M, D = x.shape
    d_ff = w1.shape[2]
    row = lambda t: (t, 0)
    return pl.pallas_call(
        functools.partial(_post_kernel, ff_chunk=min(d_ff, 1024)),
        grid=(M // tm,),
        in_specs=[pl.BlockSpec((tm, D), row), pl.BlockSpec((tm, D), row),
                  _resident(w_out, j), _resident(norm_g, i), _resident(w1, i), _resident(w2, i)],
        out_specs=pl.BlockSpec((tm, D), row),
        out_shape=jax.ShapeDtypeStruct((M, D), F32),
        scratch_shapes=[pltpu.VMEM((tm, D), F32)],
        compiler_params=_cparams("parallel"),
        name="post_mlp",
    )(x, z, w_out, norm_g, w1, w2)


def _qkv_kernel(x_ref, g_ref, w_ref, q_ref, kv_ref, tail_ref, *, tiles_per_seq, tail):
    h = _rms(x_ref[...], g_ref[0:1, :]).astype(BF16)
    p = _dot(h, w_ref[...])
    D = q_ref.shape[1]
    q_ref[...] = p[:, :D].astype(BF16)
    kv_ref[...] = p[:, D:].astype(BF16)
    tm = p.shape[0]

    @pl.when(pl.program_id(0) % tiles_per_seq == tiles_per_seq - 1)
    def _():
        tail_ref[0] = p[tm - tail:, D:]


def _qkv(x, norm_g, i, w, j, *, seq, tm):
    M, D = x.shape
    KV = w.shape[2] - D
    tail = min(WINDOW, seq)
    assert seq % tm == 0 and tail <= tm
    tps = seq // tm
    row = lambda i: (i, 0)
    return pl.pallas_call(
        functools.partial(_qkv_kernel, tiles_per_seq=tps, tail=tail),
        grid=(M // tm,),
        in_specs=[pl.BlockSpec((tm, D), row), _resident(norm_g, i), _resident(w, j)],
        out_specs=[pl.BlockSpec((tm, D), row), pl.BlockSpec((tm, KV), row),
                   pl.BlockSpec((1, tail, KV), lambda t: (t // tps, 0, 0))],
        out_shape=[jax.ShapeDtypeStruct((M, D), BF16), jax.ShapeDtypeStruct((M, KV), BF16),
                   jax.ShapeDtypeStruct((M // seq, tail, KV), F32)],
        compiler_params=_cparams("arbitrary"),
        name="attn_qkv",
    )(x, norm_g, w)


def _attn_kernel(sink_ref, table_ref, bucket_ref, q_ref, prev_ref, cur_ref, o_ref, kvx_ref, bias_ref, *,
                 qc, masked):
    @pl.when((pl.program_id(0) == 0) & (pl.program_id(1) == 0))
    def _():
        _attn_bias_init(table_ref, bucket_ref, bias_ref, qc=qc, masked=masked)

    def put(rows, lanes, val):
        o_ref[rows, lanes] = val

    _run(_attn_stages(sink_ref, q_ref, prev_ref, cur_ref, kvx_ref, bias_ref, put, qc=qc,
                      first_chunk=pl.program_id(1) * (q_ref.shape[0] // qc)))


def _attn_bias_init(table_ref, bucket_ref, bias_ref, *, qc, masked):
    n_var, n_pairs, _, band = bias_ref.shape
    keypos = lax.broadcasted_iota(jnp.int32, (1, band), 1)
    bucket = bucket_ref[...]
    for h in range(2 * n_pairs):
        pick = lambda b, acc: jnp.where(bucket == b, table_ref[b, h], acc)
        bias_h = lax.fori_loop(0, table_ref.shape[0], pick, jnp.zeros((qc, band), F32)) * LOG2E
        for v in range(n_var):
            hidden = keypos < (WINDOW - v * qc if masked else 0)
            bias_ref[v, h // 2, (h % 2) * qc:(h % 2 + 1) * qc, :] = jnp.where(hidden, -jnp.inf, bias_h)


def _attn_stages(sink_ref, q_ref, prev_ref, cur_ref, kvx_ref, bias_ref, put, *, qc, first_chunk):
    tq, D = q_ref.shape
    kvw = cur_ref.shape[1] // 2
    band = WINDOW + qc
    n_var = bias_ref.shape[0]
    n_pairs = D // LANES
    pairs_per_kv = n_pairs // (kvw // LANES)
    kvx_ref[0:WINDOW, :] = prev_ref[...]
    kvx_ref[WINDOW:, :] = cur_ref[...]
    lane = lax.broadcasted_iota(jnp.int32, (1, LANES), 1)
    low = lane < HEAD_DIM
    row2 = lax.broadcasted_iota(jnp.int32, (2 * qc, 1), 0)
    pairs = range(n_pairs)
    sink = [jnp.where(row2 < qc, sink_ref[2 * p], sink_ref[2 * p + 1]) * LOG2E for p in pairs]

    for c in range(tq // qc):
        qrows = q_ref[c * qc:(c + 1) * qc, :]
        kvb = kvx_ref[c * qc:c * qc + band, :]
        var = jnp.minimum(first_chunk + c, n_var - 1)

        def scores(p):
            qp = qrows[:, p * LANES:(p + 1) * LANES]
            zero = jnp.zeros_like(qp)
            lhs = jnp.concatenate([jnp.where(low, qp, zero), jnp.where(low, zero, qp)], axis=0)
            hk = p // pairs_per_kv
            return _dot_nt(lhs, kvb[:, hk * LANES:(hk + 1) * LANES]) + bias_ref[var, p]

        s = [scores(p) for p in pairs]
        m = [jnp.maximum(jnp.max(s[p], axis=-1, keepdims=True), sink[p]) for p in pairs]
        e = [jnp.exp2(s[p] - m[p]) for p in pairs]
        den = [jnp.sum(e[p], axis=-1, keepdims=True) + jnp.exp2(sink[p] - m[p]) for p in pairs]
        yield
        o2 = [_dot(e[p].astype(BF16),
                   kvb[:, kvw + (p // pairs_per_kv) * LANES: kvw + (p // pairs_per_kv + 1) * LANES]) for p in pairs]
        for p in pairs:
            o = o2[p] / den[p]
            put(slice(c * qc, (c + 1) * qc), slice(p * LANES, (p + 1) * LANES),
                jnp.where(low, o[:qc], o[qc:]).astype(BF16))
        yield


def _bucket_map(qc):
    qi = jnp.arange(qc)[:, None]
    kj = jnp.arange(WINDOW + qc)[None, :]
    return _rel_bucket(kj - WINDOW - qi)


def _attn(q, kv, prev, prev_map, table, sinks, *, nseq, seq, tq, qc, masked):
    M, D = q.shape
    KV2 = kv.shape[1]
    npt = seq // tq
    n_var = WINDOW // qc + 1 if masked else 1
    bucket = _bucket_map(qc)
    row = lambda b, t: (b * npt + t, 0)
    smem = pl.BlockSpec(memory_space=pltpu.SMEM)
    return pl.pallas_call(
        functools.partial(_attn_kernel, qc=qc, masked=masked),
        grid=(nseq, npt),
        in_specs=[smem, smem, _resident(bucket),
                  pl.BlockSpec((tq, D), row),
                  pl.BlockSpec((WINDOW, KV2), prev_map),
                  pl.BlockSpec((tq, KV2), row)],
        out_specs=pl.BlockSpec((tq, D), row),
        out_shape=jax.ShapeDtypeStruct((M, D), BF16),
        scratch_shapes=[pltpu.VMEM((WINDOW + tq, KV2), BF16),
                        pltpu.VMEM((n_var, D // LANES, 2 * qc, WINDOW + qc), F32)],
        compiler_params=_cparams("arbitrary", "arbitrary"),
        name="attn_core",
    )(sinks, table, bucket, q, prev, kv)


def _attn_post_kernel(sink_ref, table_ref, bucket_ref, q_ref, prev_ref, cur_ref, x_ref, wo_ref, g_ref, w1_ref,
                      w2_ref, out_ref, kvx_ref, bias_ref, o_scr, acc_ref, *, qc, tiles_per_seq, n_tiles, ff_chunk):
    t = pl.program_id(0)

    @pl.when(t == 0)
    def _():
        _attn_bias_init(table_ref, bucket_ref, bias_ref, qc=qc, masked=True)
        o_scr[...] = jnp.zeros_like(o_scr)

    tile = jnp.minimum(t, n_tiles - 1)
    slot_w = t % 2
    slot_r = (t + 1) % 2

    def put(rows, lanes, val):
        o_scr[slot_w, rows, lanes] = val

    attn = _attn_stages(sink_ref, q_ref, prev_ref, cur_ref, kvx_ref, bias_ref, put, qc=qc,
                        first_chunk=(tile % tiles_per_seq) * (q_ref.shape[0] // qc))
    post = _post_stages(x_ref, lambda: o_scr[slot_r], wo_ref, g_ref, w1_ref, w2_ref, out_ref, acc_ref,
                        ff_chunk=ff_chunk)
    live = [post, attn]
    while live:
        live = [gen for gen in live if not _step(gen)[0]]


def _attn_post(q, kv, x, table, sinks, w_out, j, norm_g, w1, w2, i, *, seq, tq, qc):
    M, D = q.shape
    KV2 = kv.shape[1]
    d_ff = w1.shape[2]
    n_tiles = M // tq
    n_var = WINDOW // qc + 1
    bucket = _bucket_map(qc)
    cur = lambda t: (jnp.minimum(t, n_tiles - 1), 0)
    prev = lambda t: (jnp.maximum(jnp.minimum(t, n_tiles - 1) * (tq // WINDOW) - 1, 0), 0)
    lag = lambda t: (jnp.maximum(t - 1, 0), 0)
    smem = pl.BlockSpec(memory_space=pltpu.SMEM)
    return pl.pallas_call(
        functools.partial(_attn_post_kernel, qc=qc, tiles_per_seq=seq // tq, n_tiles=n_tiles,
                          ff_chunk=min(d_ff, 512)),
        grid=(n_tiles + 1,),
        in_specs=[smem, smem, _resident(bucket),
                  pl.BlockSpec((tq, D), cur), pl.BlockSpec((WINDOW, KV2), prev), pl.BlockSpec((tq, KV2), cur),
                  pl.BlockSpec((tq, D), lag),
                  _resident(w_out, j), _resident(norm_g, i), _resident(w1, i), _resident(w2, i)],
        out_specs=pl.BlockSpec((tq, D), lag),
        out_shape=jax.ShapeDtypeStruct((M, D), F32),
        scratch_shapes=[pltpu.VMEM((WINDOW + tq, KV2), BF16),
                        pltpu.VMEM((n_var, D // LANES, 2 * qc, WINDOW + qc), F32),
                        pltpu.VMEM((2, tq, D), BF16),
                        pltpu.VMEM((tq, D), F32)],
        compiler_params=_cparams("arbitrary"),
        name="attn_post_mlp",
    )(sinks, table, bucket, q, kv, kv, x, w_out, norm_g, w1, w2)


def _rel_bucket(rp):
    nb = NUM_BUCKETS // 2
    ret = (rp > 0).astype(jnp.int32) * nb
    n = jnp.abs(rp)
    max_exact = nb // 2
    nf = jnp.maximum(n, 1).astype(F32)
    large = max_exact + (jnp.log(nf / max_exact) / math.log(MAX_DISTANCE / max_exact)
                         * (nb - max_exact)).astype(jnp.int32)
    large = jnp.minimum(large, nb - 1)
    return ret + jnp.where(n < max_exact, n, large)


def _dup_heads(t, n_kv):
    lead = t.shape[:-1]
    t = t.reshape(lead + (n_kv, 1, HEAD_DIM))
    return jnp.broadcast_to(t, lead + (n_kv, 2, HEAD_DIM)).reshape(lead + (n_kv * LANES,))


def _conv_kernel(x_ref, g_ref, w_ref, cw_ref, u0_ref, z_ref, un_ref, carry_ref):
    @pl.when(pl.program_id(1) == 0)
    def _():
        carry_ref[...] = u0_ref[0]

    D = x_ref.shape[1]
    h = _rms(x_ref[...], g_ref[0:1, :]).astype(BF16)
    p = _dot(h, w_ref[...])
    bg, u = p[:, :D], p[:, D:2 * D] * p[:, 2 * D:]
    tm = u.shape[0]
    row = lax.broadcasted_iota(jnp.int32, (tm, 1), 0)
    c0, c1 = carry_ref[0:1, :], carry_ref[1:2, :]
    um1 = jnp.where(row == 0, c1, pltpu.roll(u, 1, axis=0))
    um2 = jnp.where(row == 0, c0, jnp.where(row == 1, c1, pltpu.roll(u, 2, axis=0)))
    cw = cw_ref[...]
    y = um2 * cw[0:1] + um1 * cw[1:2] + u * cw[2:3]
    z_ref[...] = (bg * y).astype(BF16)
    carry_ref[...] = u[tm - 2:, :]
    un_ref[0] = u[tm - 2:, :]


def _conv(x, norm_g, i, w_in, cw, j, u0, *, nseq, seq, tm):
    M, D = x.shape
    npt = seq // tm
    row = lambda b, t: (b * npt + t, 0)
    st = lambda b, t: (b, 0, 0)
    return pl.pallas_call(
        _conv_kernel,
        grid=(nseq, npt),
        in_specs=[pl.BlockSpec((tm, D), row), _resident(norm_g, i), _resident(w_in, j),
                  _resident(cw, j), pl.BlockSpec((1, CONV_WIDTH - 1, D), st)],
        out_specs=[pl.BlockSpec((tm, D), row), pl.BlockSpec((1, CONV_WIDTH - 1, D), st)],
        out_shape=[jax.ShapeDtypeStruct((M, D), BF16),
                   jax.ShapeDtypeStruct((nseq, CONV_WIDTH - 1, D), F32)],
        scratch_shapes=[pltpu.VMEM((CONV_WIDTH - 1, D), F32)],
        compiler_params=_cparams("arbitrary", "arbitrary"),
        name="conv_mix",
    )(x, norm_g, w_in, cw, u0)


def _rwkv_proj_kernel(x_ref, g_ref, mu_ref, wrkv_ref, vec_ref, w1_ref, w2_ref, a1_ref, a2_ref,
                      g1_ref, g2_ref, sh0_ref,
                      r_ref, k_ref, v_ref, lw_ref, a_ref, gate_ref, sh_ref, carry_ref, *, sub):
    @pl.when(pl.program_id(1) == 0)
    def _():
        carry_ref[...] = sh0_ref[0]

    tm = x_ref.shape[0]
    mu = mu_ref[...]
    vec = vec_ref[...]
    row = lax.broadcasted_iota(jnp.int32, (sub, 1), 0)

    def mixes(s, prev_last):
        h = _rms(x_ref[s * sub:(s + 1) * sub, :], g_ref[0:1, :])
        xx = jnp.where(row == 0, prev_last, pltpu.roll(h, 1, axis=0)) - h
        return h[sub - 1:, :], [(h + xx * mu[i:i + 1]).astype(BF16) for i in range(6)]

    def project(s, m):
        rows = slice(s * sub, (s + 1) * sub)
        r_ref[rows, :] = _dot(m[0], wrkv_ref[0]).astype(BF16)
        k_ref[rows, :] = _dot(m[2], wrkv_ref[1])
        v_ref[rows, :] = _dot(m[3], wrkv_ref[2]).astype(BF16)
        wl = vec[0:1] + _dot(jnp.tanh(_dot(m[1], w1_ref[...])).astype(BF16), w2_ref[...])
        t = -wl
        sp = jnp.maximum(t, 0.0) + jnp.log1p(jnp.exp(-jnp.abs(t)))
        lw_ref[rows, :] = -jnp.exp(-sp - 0.5)
        a_ref[rows, :] = _sigmoid(vec[1:2] + _dot(_dot(m[4], a1_ref[...]).astype(BF16), a2_ref[...]))
        gate_ref[rows, :] = _dot(_sigmoid(_dot(m[5], g1_ref[...])).astype(BF16), g2_ref[...]).astype(BF16)

    last, m = mixes(0, carry_ref[...])
    for s in range(tm // sub):
        nxt = mixes(s + 1, last) if (s + 1) * sub < tm else None
        project(s, m)
        if nxt is not None:
            last, m = nxt
    carry_ref[...] = last
    sh_ref[0] = last


def _rwkv_proj(x, norm_g, i, W, j, sh0, *, nseq, seq, tm, sub):
    M, D = x.shape
    npt = seq // tm
    row = lambda b, t: (b * npt + t, 0)
    st = lambda b, t: (b, 0, 0)
    big = pl.BlockSpec((tm, D), row)
    sd = lambda dt: jax.ShapeDtypeStruct((M, D), dt)
    names = ('b_mu', 'b_w_rkv', 'b_vec', 'b_w1', 'b_w2', 'b_a1', 'b_a2', 'b_g1', 'b_g2')
    return pl.pallas_call(
        functools.partial(_rwkv_proj_kernel, sub=sub),
        grid=(nseq, npt),
        in_specs=[big, _resident(norm_g, i)] + [_resident(W[n], j) for n in names]
                 + [pl.BlockSpec((1, 1, D), st)],
        out_specs=[big, big, big, big, big, big, pl.BlockSpec((1, 1, D), st)],
        out_shape=[sd(BF16), sd(F32), sd(BF16), sd(F32), sd(F32), sd(BF16),
                   jax.ShapeDtypeStruct((nseq, 1, D), F32)],
        scratch_shapes=[pltpu.VMEM((1, D), F32)],
        compiler_params=_cparams("arbitrary", "arbitrary"),
        name="rwkv_proj",
    )(x, norm_g, *[W[n] for n in names], sh0)


def _wkv_chunk(r, k, v, lw, a, gate, prm):
    C = r[0].shape[0]
    C2 = 2 * C
    H = RWKV_HEAD
    pm = lambda f, *ls: [f(*xs) for xs in zip(*ls)]
    bf = lambda xs: [x.astype(BF16) for x in xs]
    lane = lax.broadcasted_iota(jnp.int32, (1, LANES), 1)
    low = lane < H
    hr = lax.broadcasted_iota(jnp.int32, (LANES, LANES), 0)
    hc = lax.broadcasted_iota(jnp.int32, (LANES, LANES), 1)
    head_bd = (hr < H) == (hc < H)
    ones_bd = jnp.where(head_bd, 1.0, 0.0).astype(BF16)
    fold = lambda x: x[:C] + x[C:]

    def segsum(xs):
        return [_dot(x.astype(BF16), ones_bd) for x in xs]

    def stack(x):
        z = jnp.zeros_like(x)
        return jnp.concatenate([jnp.where(low, x, z), jnp.where(low, z, x)], axis=0)

    def hilo_cols(x):
        hi = x.astype(BF16)
        return jnp.concatenate([hi, (x - hi.astype(F32)).astype(BF16)], axis=1)

    def prep():
        tr = lax.broadcasted_iota(jnp.int32, (C, C), 0)
        tc = lax.broadcasted_iota(jnp.int32, (C, C), 1)
        tri = jnp.where(tr >= tc, 1.0, 0.0).astype(BF16)
        cum2 = [_dot(tri, x) for x in pm(hilo_cols, lw)]
        kkr = pm(lambda k_, p_: k_ * p_[0:1], k, prm)
        ss = segsum([x * x for x in kkr])
        yield
        cum = [c2[:, :LANES] + c2[:, LANES:] for c2 in cum2]
        G = [jnp.exp(c) for c in cum]
        Gm1 = pm(lambda c, l: jnp.exp(c - l), cum, lw)
        iG = [jnp.exp(-c) for c in cum]
        G_end = [jnp.exp(c[C - 1:C, :]) for c in cum]
        G_rest = [jnp.exp(c[C - 1:C, :] - c) for c in cum]
        kk = pm(lambda x, s_: x * lax.rsqrt(jnp.maximum(s_, 1e-24)), kkr, ss)
        k2 = pm(lambda k_, a_, p_: k_ * (1.0 + (a_ - 1.0) * p_[1:2]), k, a, prm)
        b = pm(lambda x, a_: x * a_, kk, a)
        At = pm(lambda x, g_: -x * g_, kk, Gm1)
        Rt = pm(lambda x, g_: x * g_, r, G)

        tt = lax.broadcasted_iota(jnp.int32, (C, C2), 0)
        ts = lax.broadcasted_iota(jnp.int32, (C, C2), 1) & (C - 1)
        strict = tt > ts
        incl = tt >= ts
        first = lax.broadcasted_iota(jnp.int32, (1, C2), 1) < C

        def bdiag(x):
            z = jnp.zeros_like(x)
            return jnp.concatenate([jnp.where(first, x, z), jnp.where(first, z, x)], axis=0).astype(BF16)

        lhs_s = pm(lambda x, y: jnp.concatenate([x, y], axis=0).astype(BF16), At, Rt)
        Bt = pm(lambda x, g_: stack(x * g_).astype(BF16), b, iG)
        Kt = pm(lambda x, g_: stack(x * g_).astype(BF16), k2, iG)
        P = pm(lambda l_, b_, k_: _dot_nt(l_, jnp.concatenate([b_, k_], axis=0)), lhs_s, Bt, Kt)
        rk = segsum(pm(lambda r_, k_, p_: r_ * k_ * p_[2:3], r, k2, prm))
        yield
        Lab = [jnp.where(strict, x[:C, :C2], 0.0) for x in P]
        Lak = [jnp.where(strict, x[:C, C2:], 0.0).astype(BF16) for x in P]
        Lrr = [jnp.concatenate([jnp.where(incl, x[C:, :C2], 0.0), jnp.where(incl, x[C:, C2:], 0.0)],
                               axis=1).astype(BF16) for x in P]
        Vst = [stack(x).astype(BF16) for x in v]
        LV = pm(_dot, Lak, Vst)

        blk = lambda s: (tt >> int(math.log2(s))) == (ts >> int(math.log2(s)))
        eye = jnp.where(tt == ts, 1.0, 0.0)
        L1 = [jnp.where(blk(8), x, 0.0) for x in Lab]
        L2 = pm(lambda x: _dot(x.astype(BF16), bdiag(x)), L1)
        yield
        L4 = pm(lambda x: _dot(x.astype(BF16), bdiag(x)), L2)
        T = pm(lambda x, y: _dot((eye + x).astype(BF16), bdiag(eye + y)), L1, L2)
        yield
        T = pm(lambda x, y: _dot(x.astype(BF16), bdiag(eye + y)), T, L4)
        yield
        s = 8
        while s < C:
            msk = blk(2 * s) & jnp.logical_not(blk(s))
            upper = [(o + s, o + 2 * s) for o in range(0, C, 2 * s)]
            Mx = [bdiag(jnp.where(msk, x, 0.0)) for x in Lab]
            Tu = [jnp.concatenate([t[a_:b_] for a_, b_ in upper], axis=0).astype(BF16) for t in T]
            TM = pm(_dot, Tu, Mx)
            yield
            X = pm(lambda tm_, t: _dot(tm_.astype(BF16), bdiag(t)), TM, T)
            yield
            T = [jnp.concatenate([piece for n, (a_, b_) in enumerate(upper)
                                  for piece in (t[a_ - s:a_], t[a_:b_] + x[n * s:(n + 1) * s])], axis=0)
                 for t, x in zip(T, X)]
            s *= 2
        rhs_s = pm(lambda b_, k_, g_: jnp.concatenate([b_ * g_, k_ * g_], axis=0).astype(BF16), b, k2, G_rest)
        return dict(Tb=bf(T), lhs_s=lhs_s, rhs_s=rhs_s, LV=LV, Lrr=Lrr, Vst=Vst, G_end=G_end, rk=rk)

    def apply(q, S):
        AZ = pm(lambda l_, s_: _dot_nt(l_, s_.astype(BF16)), q['lhs_s'], S)
        yield
        U = pm(lambda t_, az, lv: _dot(t_, stack(az[:C] + lv).astype(BF16)), q['Tb'], AZ, q['LV'])
        yield
        Y = pm(lambda az, l_, u_, v_: az[C:] + _dot(l_, jnp.concatenate([stack(u_).astype(BF16), v_], axis=0)),
               AZ, q['Lrr'], U, q['Vst'])
        upd = pm(lambda u_, v_, rhs_: _dot_tn(jnp.concatenate([u_, v_], axis=0).astype(BF16), rhs_),
                 U, v, q['rhs_s'])
        yield
        S_new = pm(lambda s_, g_, u_: s_ * g_ + jnp.where(head_bd, u_, 0.0), S, q['G_end'], upd)
        mean = [x * (1.0 / H) for x in segsum(Y)]
        yield
        d = pm(lambda y_, m_: y_ - m_, Y, mean)
        var = [x * (1.0 / H) for x in segsum([x * x for x in d])]
        yield
        out = pm(lambda d_, var_, p_, rk_, v_, g_:
                 (d_ * lax.rsqrt(var_ + GN_EPS) * p_[3:4] + p_[4:5] + rk_ * v_) * g_,
                 d, var, prm, q['rk'], v, gate)
        return out, S_new

    return prep, apply


PREP_AHEAD = 2


def _wkv_kernel(r_ref, k_ref, v_ref, lw_ref, a_ref, g_ref, prm_ref, s0_ref, z_ref, sf_ref, *, C):
    @pl.when(pl.program_id(1) == 0)
    def _():
        sf_ref[...] = s0_ref[...]

    tb, D = r_ref.shape
    n_pairs = D // LANES
    n_chunks = tb // C
    lanes = [slice(p * LANES, (p + 1) * LANES) for p in range(n_pairs)]
    prm = [prm_ref[:, ln] for ln in lanes]

    def stages(c):
        rows = slice(c * C, (c + 1) * C)
        ld = lambda ref: [ref[rows, ln].astype(F32) for ln in lanes]
        return _wkv_chunk(ld(r_ref), ld(k_ref), ld(v_ref), ld(lw_ref), ld(a_ref), ld(g_ref), prm)

    S = [sf_ref[0, p] for p in range(n_pairs)]
    applies, ready, in_flight = {}, {}, []
    launched = done = 0
    cur = None
    while done < n_chunks:
        while len(in_flight) < PREP_AHEAD and launched < n_chunks:
            prep, applies[launched] = stages(launched)
            in_flight.append((launched, prep()))
            launched += 1
        if cur is None and done in ready:
            cur = applies.pop(done)(ready.pop(done), S)
        if cur is not None:
            finished, res = _step(cur)
            if finished:
                z, S = res
                for p, ln in enumerate(lanes):
                    z_ref[done * C:(done + 1) * C, ln] = z[p].astype(BF16)
                cur = None
                done += 1
        for c, gen in list(in_flight):
            finished, res = _step(gen)
            if finished:
                ready[c] = res
                in_flight.remove((c, gen))
    for p in range(n_pairs):
        sf_ref[0, p] = S[p]


def _wkv(r, k, v, lw, a, gate, prm, j, s0, *, nseq, seq, tb, C):
    M, D = r.shape
    npt = seq // tb
    row = lambda b, t: (b * npt + t, 0)
    big = pl.BlockSpec((tb, D), row)
    st = pl.BlockSpec((1,) + s0.shape[1:], lambda b, t: (b, 0, 0, 0))
    return pl.pallas_call(
        functools.partial(_wkv_kernel, C=C),
        grid=(nseq, npt),
        in_specs=[big, big, big, big, big, big, _resident(prm, j), st],
        out_specs=[big, st],
        out_shape=[jax.ShapeDtypeStruct((M, D), BF16), jax.ShapeDtypeStruct(s0.shape, F32)],
        compiler_params=_cparams("arbitrary", "arbitrary"),
        name="rwkv_wkv",
    )(r, k, v, lw, a, gate, prm, s0)


def _pair_state(s):
    B, Hh, N, _ = s.shape
    s = s.reshape(B, Hh // 2, 2, N, N)
    z = jnp.zeros((B, Hh // 2, N, N), s.dtype)
    return jnp.concatenate([jnp.concatenate([s[:, :, 0], z], axis=-1),
                            jnp.concatenate([z, s[:, :, 1]], axis=-1)], axis=-2)


def _unpair_state(z):
    N = z.shape[-1] // 2
    return jnp.stack([z[:, :, :N, :N], z[:, :, N:, N:]], axis=2).reshape(z.shape[0], -1, N, N)


TILE_ROWS = dict(post=512, seq=512, rwkv_proj=512, rwkv_sub=256, wkv=256)


def _tile(n, pref):
    t = min(n, pref)
    assert n % t == 0
    return t


def _trunk(x3, prompt, a_k, a_v, b_wkv, b_shift, c_conv, W):
    B, T, D = x3.shape
    M = B * T
    x = x3.reshape(M, D)
    norm_g = W['norm_g']
    n_heads = D // HEAD_DIM
    n_kv = W['n_kv']
    kvw = n_kv * LANES
    nk, nv, nwkv, nsh, ncv = [], [], [], [], []
    seq_tile = _tile(T, TILE_ROWS['seq'])
    for i in range(norm_g.shape[0]):
        kind, j = i % N_MIXERS, i // N_MIXERS
        if kind == 0:
            q, kv, tail = _qkv(x, norm_g, i, W['a_w_qkv'], j, seq=T, tm=seq_tile)
            tail_k = tail[:, :, :kvw].reshape(B, -1, n_kv, 2, HEAD_DIM)[:, :, :, 0]
            tail_v = tail[:, :, kvw:].reshape(B, -1, n_kv, 2, HEAD_DIM)[:, :, :, 0]
            if prompt:
                nk.append(tail_k)
                nv.append(tail_v)
                x = _attn_post(q, kv, x, W['rel_bias_table'], W['a_sinks'][j], W['a_w_o'], j, norm_g,
                               W['mlp_w1'], W['mlp_w2'], i, seq=T, tq=seq_tile, qc=CHUNK)
                continue
            ck, cv = a_k[j], a_v[j]
            prev = jnp.concatenate([_dup_heads(ck.reshape(B, WINDOW, -1), n_kv),
                                    _dup_heads(cv.reshape(B, WINDOW, -1), n_kv)],
                                   axis=-1).astype(BF16).reshape(B * WINDOW, 2 * kvw)
            z = _attn(q, kv, prev, lambda b, t: (b, 0), W['rel_bias_table'], W['a_sinks'][j], nseq=B, seq=T,
                      tq=seq_tile, qc=T, masked=False)
            nk.append(jnp.concatenate([ck, tail_k], axis=1)[:, -WINDOW:])
            nv.append(jnp.concatenate([cv, tail_v], axis=1)[:, -WINDOW:])
            w_out = W['a_w_o']
        elif kind == 1:
            if prompt:
                sh0 = jnp.zeros((B, 1, D), F32)
                s0 = jnp.zeros((B, n_heads // 2, LANES, LANES), F32)
            else:
                sh0 = b_shift[j].reshape(B, 1, D)
                s0 = _pair_state(b_wkv[j])
            tm = _tile(T, TILE_ROWS['rwkv_proj'])
            r, k, v, lw, a, gate, sh_new = _rwkv_proj(x, norm_g, i, W, j, sh0, nseq=B, seq=T, tm=tm,
                                                      sub=_tile(tm, TILE_ROWS['rwkv_sub']))
            z, s_fin = _wkv(r, k, v, lw, a, gate, W['b_prm'], j, s0, nseq=B, seq=T,
                            tb=_tile(T, TILE_ROWS['wkv']), C=min(T, CHUNK))
            nwkv.append(_unpair_state(s_fin))
            nsh.append(sh_new.reshape(B, D))
            w_out = W['b_w_o']
        else:
            u0 = jnp.zeros((B, CONV_WIDTH - 1, D), F32) if prompt else c_conv[j]
            z, u_new = _conv(x, norm_g, i, W['c_w_in'], W['c_conv_w'], j, u0, nseq=B, seq=T, tm=seq_tile)
            ncv.append(u_new)
            w_out = W['c_w_out']
        x = _post(x, z, w_out, j, norm_g, W['mlp_w1'], W['mlp_w2'], i, tm=_tile(M, TILE_ROWS['post']))
    return (x.reshape(B, T, D), jnp.stack(nk), jnp.stack(nv), jnp.stack(nwkv), jnp.stack(nsh),
            jnp.stack(ncv))


def kernel(x_prompt, x_sample, cache_a_k, cache_a_v, state_b_wkv, state_b_shift, state_c_conv, rel_bias_table, norm_g, a_w_qkv, a_w_o, a_sinks, b_mu, b_w_rkv, b_w_o, b_w0, b_w1, b_w2, b_a0, b_a1, b_a2, b_g1, b_g2, b_k_k, b_k_a, b_r_k, b_ln_w, b_ln_b, c_w_in, c_conv_w, c_w_out, mlp_w1, mlp_w2):
    D = x_prompt.shape[-1]
    n_kv = cache_a_k.shape[3]
    nq = D
    nkv = n_kv * HEAD_DIM
    bf = lambda t: t.astype(BF16)
    wq, wk, wv = a_w_qkv[..., :nq], a_w_qkv[..., nq:nq + nkv], a_w_qkv[..., nq + nkv:]
    w_qkv = bf(jnp.concatenate([wq * (HEAD_DIM ** -0.5 * LOG2E), _dup_heads(wk, n_kv), _dup_heads(wv, n_kv)],
                               axis=-1))
    n_b = b_mu.shape[0]
    zeros = jnp.zeros_like(b_w0)
    W = dict(
        n_kv=n_kv, rel_bias_table=rel_bias_table, norm_g=norm_g, a_w_qkv=w_qkv, a_w_o=bf(a_w_o), a_sinks=a_sinks,
        b_mu=b_mu, b_w_rkv=bf(b_w_rkv), b_w_o=bf(b_w_o),
        b_vec=jnp.stack([b_w0, b_a0], axis=1),
        b_w1=bf(b_w1), b_w2=bf(b_w2), b_a1=bf(b_a1), b_a2=bf(b_a2), b_g1=bf(b_g1), b_g2=bf(b_g2),
        b_prm=jnp.stack([b_k_k, b_k_a, b_r_k.reshape(n_b, D), b_ln_w, b_ln_b, zeros, zeros, zeros], axis=1),
        c_w_in=bf(c_w_in), c_conv_w=c_conv_w, c_w_out=bf(c_w_out), mlp_w1=bf(mlp_w1), mlp_w2=bf(mlp_w2))
    y_p, ak_p, av_p, wkv_p, sh_p, cv_p = _trunk(x_prompt, True, None, None, None, None, None, W)
    y_s, ak_s, av_s, wkv_s, sh_s, cv_s = _trunk(x_sample, False, cache_a_k, cache_a_v, state_b_wkv,
                                                state_b_shift, state_c_conv, W)
    return (y_p, y_s, ak_p, av_p, ak_s, av_s, wkv_p, wkv_s, sh_p, sh_s, cv_p, cv_s)
```

```python
import functools
import math

import jax
import jax.numpy as jnp
from jax import lax
from jax.experimental import pallas as pl
from jax.experimental.pallas import tpu as pltpu

F32 = jnp.float32
BF16 = jnp.bfloat16

HEAD_DIM = 64
CHUNK = 64
WINDOW = 128
NUM_BUCKETS = 32
MAX_DISTANCE = 128
RWKV_HEAD = 64
CONV_WIDTH = 3
N_MIXERS = 3
RMS_EPS = 1e-6
GN_EPS = RWKV_HEAD * 1e-5
LOG2E = math.log2(math.e)

LANES = 128
VMEM_LIMIT = 56 * 1024 * 1024


def _cparams(*sem):
    return pltpu.CompilerParams(dimension_semantics=sem, vmem_limit_bytes=VMEM_LIMIT)


def _resident(arr, layer=None):
    if layer is None:
        nd = arr.ndim
        return pl.BlockSpec(arr.shape, lambda *_: (0,) * nd, pipeline_mode=pl.Buffered(1))
    nd = arr.ndim - 1
    return pl.BlockSpec((None,) + arr.shape[1:], lambda *_: (layer,) + (0,) * nd, pipeline_mode=pl.Buffered(1))


def _rms(x, g):
    return x * lax.rsqrt(jnp.mean(x * x, axis=-1, keepdims=True) + RMS_EPS) * g


def _dot(a, b):
    return jnp.dot(a, b, preferred_element_type=F32)


def _dot_nt(a, b):
    return lax.dot_general(a, b, (((1,), (1,)), ((), ())), preferred_element_type=F32)


def _dot_tn(a, b):
    return lax.dot_general(a, b, (((0,), (0,)), ((), ())), preferred_element_type=F32)


def _sigmoid(x):
    return 1.0 / (1.0 + jnp.exp(-x))


def _step(gen):
    try:
        next(gen)
        return False, None
    except StopIteration as stop:
        return True, stop.value


def _run(*gens, strides=None):
    strides = strides or [1] * len(gens)
    results = [None] * len(gens)
    live = list(range(len(gens)))
    while live:
        for n in list(live):
            for _ in range(strides[n]):
                finished, value = _step(gens[n])
                if finished:
                    results[n] = value
                    live.remove(n)
                    break
    return results[0] if len(gens) == 1 else results


def _post_stages(x_ref, load_z, wo_ref, g_ref, w1_ref, w2_ref, o_ref, acc_ref, *, ff_chunk):
    g = g_ref[...]
    m = _dot(load_z(), wo_ref[...])
    x1 = x_ref[...] + _rms(m, g[1:2])
    h2 = _rms(x1, g[2:3]).astype(BF16)
    yield
    d_ff = w1_ref.shape[1]
    for c in range(d_ff // ff_chunk):
        sl = slice(c * ff_chunk, (c + 1) * ff_chunk)
        a = jnp.maximum(_dot(h2, w1_ref[:, sl]), 0.0)
        a = (a * a).astype(BF16)
        yield
        part = _dot(a, w2_ref[sl, :])
        if c == 0:
            acc_ref[...] = part
        else:
            acc_ref[...] += part
        yield
    o_ref[...] = x1 + _rms(acc_ref[...], g[3:4])


def _post_kernel(x_ref, z_ref, wo_ref, g_ref, w1_ref, w2_ref, o_ref, acc_ref, *, ff_chunk):
    _run(_post_stages(x_ref, lambda: z_ref[...], wo_ref, g_ref, w1_ref, w2_ref, o_ref, acc_ref,
                      ff_chunk=ff_chunk))


def _post(x, z, w_out, j, norm_g, w1, w2, i, *, tm):
    M, D = x.shape
    d_ff = w1.shape[2]
    row = lambda t: (t, 0)
    return pl.pallas_call(
        functools.partial(_post_kernel, ff_chunk=min(d_ff, 1024)),
        grid=(M // tm,),
        in_specs=[pl.BlockSpec((tm, D), row), pl.BlockSpec((tm, D), row),
                  _resident(w_out, j), _resident(norm_g, i), _resident(w1, i), _resident(w2, i)],
        out_specs=pl.BlockSpec((tm, D), row),
        out_shape=jax.ShapeDtypeStruct((M, D), F32),
        scratch_shapes=[pltpu.VMEM((tm, D), F32)],
        compiler_params=_cparams("parallel"),
        name="post_mlp",
    )(x, z, w_out, norm_g, w1, w2)


def _qkv_kernel(x_ref, g_ref, w_ref, q_ref, kv_ref, tail_ref, *, tiles_per_seq, tail):
    h = _rms(x_ref[...], g_ref[0:1, :]).astype(BF16)
    p = _dot(h, w_ref[...])
    D = q_ref.shape[1]
    q_ref[...] = p[:, :D].astype(BF16)
    kv_ref[...] = p[:, D:].astype(BF16)
    tm = p.shape[0]

    @pl.when(pl.program_id(0) % tiles_per_seq == tiles_per_seq - 1)
    def _():
        tail_ref[0] = p[tm - tail:, D:]


def _qkv(x, norm_g, i, w, j, *, seq, tm):
    M, D = x.shape
    KV = w.shape[2] - D
    tail = min(WINDOW, seq)
    assert seq % tm == 0 and tail <= tm
    tps = seq // tm
    row = lambda i: (i, 0)
    return pl.pallas_call(
        functools.partial(_qkv_kernel, tiles_per_seq=tps, tail=tail),
        grid=(M // tm,),
        in_specs=[pl.BlockSpec((tm, D), row), _resident(norm_g, i), _resident(w, j)],
        out_specs=[pl.BlockSpec((tm, D), row), pl.BlockSpec((tm, KV), row),
                   pl.BlockSpec((1, tail, KV), lambda t: (t // tps, 0, 0))],
        out_shape=[jax.ShapeDtypeStruct((M, D), BF16), jax.ShapeDtypeStruct((M, KV), BF16),
                   jax.ShapeDtypeStruct((M // seq, tail, KV), F32)],
        compiler_params=_cparams("arbitrary"),
        name="attn_qkv",
    )(x, norm_g, w)


def _attn_kernel(sink_ref, table_ref, bucket_ref, q_ref, prev_ref, cur_ref, o_ref, kvx_ref, bias_ref, *,
                 qc, masked):
    @pl.when((pl.program_id(0) == 0) & (pl.program_id(1) == 0))
    def _():
        _attn_bias_init(table_ref, bucket_ref, bias_ref, qc=qc, masked=masked)

    def put(rows, lanes, val):
        o_ref[rows, lanes] = val

    _run(_attn_stages(sink_ref, q_ref, prev_ref, cur_ref, kvx_ref, bias_ref, put, qc=qc,
                      first_chunk=pl.program_id(1) * (q_ref.shape[0] // qc)))


def _attn_bias_init(table_ref, bucket_ref, bias_ref, *, qc, masked):
    n_var, n_pairs, _, band = bias_ref.shape
    keypos = lax.broadcasted_iota(jnp.int32, (1, band), 1)
    bucket = bucket_ref[...]
    for h in range(2 * n_pairs):
        pick = lambda b, acc: jnp.where(bucket == b, table_ref[b, h], acc)
        bias_h = lax.fori_loop(0, table_ref.shape[0], pick, jnp.zeros((qc, band), F32)) * LOG2E
        for v in range(n_var):
            hidden = keypos < (WINDOW - v * qc if masked else 0)
            bias_ref[v, h // 2, (h % 2) * qc:(h % 2 + 1) * qc, :] = jnp.where(hidden, -jnp.inf, bias_h)


def _attn_stages(sink_ref, q_ref, prev_ref, cur_ref, kvx_ref, bias_ref, put, *, qc, first_chunk):
    tq, D = q_ref.shape
    kvw = cur_ref.shape[1] // 2
    band = WINDOW + qc
    n_var = bias_ref.shape[0]
    n_pairs = D // LANES
    pairs_per_kv = n_pairs // (kvw // LANES)
    kvx_ref[0:WINDOW, :] = prev_ref[...]
    kvx_ref[WINDOW:, :] = cur_ref[...]
    lane = lax.broadcasted_iota(jnp.int32, (1, LANES), 1)
    low = lane < HEAD_DIM
    row2 = lax.broadcasted_iota(jnp.int32, (2 * qc, 1), 0)
    pairs = range(n_pairs)
    sink = [jnp.where(row2 < qc, sink_ref[2 * p], sink_ref[2 * p + 1]) * LOG2E for p in pairs]

    for c in range(tq // qc):
        qrows = q_ref[c * qc:(c + 1) * qc, :]
        kvb = kvx_ref[c * qc:c * qc + band, :]
        var = jnp.minimum(first_chunk + c, n_var - 1)

        def scores(p):
            qp = qrows[:, p * LANES:(p + 1) * LANES]
            zero = jnp.zeros_like(qp)
            lhs = jnp.concatenate([jnp.where(low, qp, zero), jnp.where(low, zero, qp)], axis=0)
            hk = p // pairs_per_kv
            return _dot_nt(lhs, kvb[:, hk * LANES:(hk + 1) * LANES]) + bias_ref[var, p]

        s = [scores(p) for p in pairs]
        m = [jnp.maximum(jnp.max(s[p], axis=-1, keepdims=True), sink[p]) for p in pairs]
        e = [jnp.exp2(s[p] - m[p]) for p in pairs]
        den = [jnp.sum(e[p], axis=-1, keepdims=True) + jnp.exp2(sink[p] - m[p]) for p in pairs]
        yield
        o2 = [_dot(e[p].astype(BF16),
                   kvb[:, kvw + (p // pairs_per_kv) * LANES: kvw + (p // pairs_per_kv + 1) * LANES]) for p in pairs]
        for p in pairs:
            o = o2[p] / den[p]
            put(slice(c * qc, (c + 1) * qc), slice(p * LANES, (p + 1) * LANES),
                jnp.where(low, o[:qc], o[qc:]).astype(BF16))
        yield


def _bucket_map(qc):
    qi = jnp.arange(qc)[:, None]
    kj = jnp.arange(WINDOW + qc)[None, :]
    return _rel_bucket(kj - WINDOW - qi)


def _attn(q, kv, prev, prev_map, table, sinks, *, nseq, seq, tq, qc, masked):
    M, D = q.shape
    KV2 = kv.shape[1]
    npt = seq // tq
    n_var = WINDOW // qc + 1 if masked else 1
    bucket = _bucket_map(qc)
    row = lambda b, t: (b * npt + t, 0)
    smem = pl.BlockSpec(memory_space=pltpu.SMEM)
    return pl.pallas_call(
        functools.partial(_attn_kernel, qc=qc, masked=masked),
        grid=(nseq, npt),
        in_specs=[smem, smem, _resident(bucket),
                  pl.BlockSpec((tq, D), row),
                  pl.BlockSpec((WINDOW, KV2), prev_map),
                  pl.BlockSpec((tq, KV2), row)],
        out_specs=pl.BlockSpec((tq, D), row),
        out_shape=jax.ShapeDtypeStruct((M, D), BF16),
        scratch_shapes=[pltpu.VMEM((WINDOW + tq, KV2), BF16),
                        pltpu.VMEM((n_var, D // LANES, 2 * qc, WINDOW + qc), F32)],
        compiler_params=_cparams("arbitrary", "arbitrary"),
        name="attn_core",
    )(sinks, table, bucket, q, prev, kv)


def _attn_post_kernel(sink_ref, table_ref, bucket_ref, q_ref, prev_ref, cur_ref, x_ref, wo_ref, g_ref, w1_ref,
                      w2_ref, out_ref, kvx_ref, bias_ref, o_scr, acc_ref, *, qc, tiles_per_seq, n_tiles, ff_chunk):
    t = pl.program_id(0)

    @pl.when(t == 0)
    def _():
        _attn_bias_init(table_ref, bucket_ref, bias_ref, qc=qc, masked=True)
        o_scr[...] = jnp.zeros_like(o_scr)

    tile = jnp.minimum(t, n_tiles - 1)
    slot_w = t % 2
    slot_r = (t + 1) % 2

    def put(rows, lanes, val):
        o_scr[slot_w, rows, lanes] = val

    attn = _attn_stages(sink_ref, q_ref, prev_ref, cur_ref, kvx_ref, bias_ref, put, qc=qc,
                        first_chunk=(tile % tiles_per_seq) * (q_ref.shape[0] // qc))
    post = _post_stages(x_ref, lambda: o_scr[slot_r], wo_ref, g_ref, w1_ref, w2_ref, out_ref, acc_ref,
                        ff_chunk=ff_chunk)
    live = [post, attn]
    while live:
        live = [gen for gen in live if not _step(gen)[0]]


def _attn_post(q, kv, x, table, sinks, w_out, j, norm_g, w1, w2, i, *, seq, tq, qc):
    M, D = q.shape
    KV2 = kv.shape[1]
    d_ff = w1.shape[2]
    n_tiles = M // tq
    n_var = WINDOW // qc + 1
    bucket = _bucket_map(qc)
    cur = lambda t: (jnp.minimum(t, n_tiles - 1), 0)
    prev = lambda t: (jnp.maximum(jnp.minimum(t, n_tiles - 1) * (tq // WINDOW) - 1, 0), 0)
    lag = lambda t: (jnp.maximum(t - 1, 0), 0)
    smem = pl.BlockSpec(memory_space=pltpu.SMEM)
    return pl.pallas_call(
        functools.partial(_attn_post_kernel, qc=qc, tiles_per_seq=seq // tq, n_tiles=n_tiles,
                          ff_chunk=min(d_ff, 512)),
        grid=(n_tiles + 1,),
        in_specs=[smem, smem, _resident(bucket),
                  pl.BlockSpec((tq, D), cur), pl.BlockSpec((WINDOW, KV2), prev), pl.BlockSpec((tq, KV2), cur),
                  pl.BlockSpec((tq, D), lag),
                  _resident(w_out, j), _resident(norm_g, i), _resident(w1, i), _resident(w2, i)],
        out_specs=pl.BlockSpec((tq, D), lag),
        out_shape=jax.ShapeDtypeStruct((M, D), F32),
        scratch_shapes=[pltpu.VMEM((WINDOW + tq, KV2), BF16),
                        pltpu.VMEM((n_var, D // LANES, 2 * qc, WINDOW + qc), F32),
                        pltpu.VMEM((2, tq, D), BF16),
                        pltpu.VMEM((tq, D), F32)],
        compiler_params=_cparams("arbitrary"),
        name="attn_post_mlp",
    )(sinks, table, bucket, q, kv, kv, x, w_out, norm_g, w1, w2)


def _rel_bucket(rp):
    nb = NUM_BUCKETS // 2
    ret = (rp > 0).astype(jnp.int32) * nb
    n = jnp.abs(rp)
    max_exact = nb // 2
    nf = jnp.maximum(n, 1).astype(F32)
    large = max_exact + (jnp.log(nf / max_exact) / math.log(MAX_DISTANCE / max_exact)
                         * (nb - max_exact)).astype(jnp.int32)
    large = jnp.minimum(large, nb - 1)
    return ret + jnp.where(n < max_exact, n, large)


def _dup_heads(t, n_kv):
    lead = t.shape[:-1]
    t = t.reshape(lead + (n_kv, 1, HEAD_DIM))
    return jnp.broadcast_to(t, lead + (n_kv, 2, HEAD_DIM)).reshape(lead + (n_kv * LANES,))


def _conv_kernel(x_ref, g_ref, w_ref, cw_ref, u0_ref, z_ref, un_ref, carry_ref):
    @pl.when(pl.program_id(1) == 0)
    def _():
        carry_ref[...] = u0_ref[0]

    D = x_ref.shape[1]
    h = _rms(x_ref[...], g_ref[0:1, :]).astype(BF16)
    p = _dot(h, w_ref[...])
    bg, u = p[:, :D], p[:, D:2 * D] * p[:, 2 * D:]
    tm = u.shape[0]
    row = lax.broadcasted_iota(jnp.int32, (tm, 1), 0)
    c0, c1 = carry_ref[0:1, :], carry_ref[1:2, :]
    um1 = jnp.where(row == 0, c1, pltpu.roll(u, 1, axis=0))
    um2 = jnp.where(row == 0, c0, jnp.where(row == 1, c1, pltpu.roll(u, 2, axis=0)))
    cw = cw_ref[...]
    y = um2 * cw[0:1] + um1 * cw[1:2] + u * cw[2:3]
    z_ref[...] = (bg * y).astype(BF16)
    carry_ref[...] = u[tm - 2:, :]
    un_ref[0] = u[tm - 2:, :]


def _conv(x, norm_g, i, w_in, cw, j, u0, *, nseq, seq, tm):
    M, D = x.shape
    npt = seq // tm
    row = lambda b, t: (b * npt + t, 0)
    st = lambda b, t: (b, 0, 0)
    return pl.pallas_call(
        _conv_kernel,
        grid=(nseq, npt),
        in_specs=[pl.BlockSpec((tm, D), row), _resident(norm_g, i), _resident(w_in, j),
                  _resident(cw, j), pl.BlockSpec((1, CONV_WIDTH - 1, D), st)],
        out_specs=[pl.BlockSpec((tm, D), row), pl.BlockSpec((1, CONV_WIDTH - 1, D), st)],
        out_shape=[jax.ShapeDtypeStruct((M, D), BF16),
                   jax.ShapeDtypeStruct((nseq, CONV_WIDTH - 1, D), F32)],
        scratch_shapes=[pltpu.VMEM((CONV_WIDTH - 1, D), F32)],
        compiler_params=_cparams("arbitrary", "arbitrary"),
        name="conv_mix",
    )(x, norm_g, w_in, cw, u0)


PROJ_OUT = (('r', BF16), ('k', F32), ('v', BF16), ('lw', F32), ('a', F32), ('gate', BF16))


def _proj_stages(x_ref, g_ref, mu_ref, wrkv_ref, vec_ref, w1_ref, w2_ref, a1_ref, a2_ref, g1_ref, g2_ref,
                 carry_ref, put, *, sub):
    tm = x_ref.shape[0]
    mu = mu_ref[...]
    vec = vec_ref[...]
    row = lax.broadcasted_iota(jnp.int32, (sub, 1), 0)

    def mixes(s, prev_last):
        h = _rms(x_ref[s * sub:(s + 1) * sub, :], g_ref[0:1, :])
        xx = jnp.where(row == 0, prev_last, pltpu.roll(h, 1, axis=0)) - h
        return h[sub - 1:, :], [(h + xx * mu[i:i + 1]).astype(BF16) for i in range(6)]

    last, m = mixes(0, carry_ref[...])
    yield
    for s in range(tm // sub):
        rows = slice(s * sub, (s + 1) * sub)
        nxt = mixes(s + 1, last) if (s + 1) * sub < tm else None
        put('r', rows, _dot(m[0], wrkv_ref[0]).astype(BF16))
        yield
        put('k', rows, _dot(m[2], wrkv_ref[1]))
        yield
        put('v', rows, _dot(m[3], wrkv_ref[2]).astype(BF16))
        yield
        wl = vec[0:1] + _dot(jnp.tanh(_dot(m[1], w1_ref[...])).astype(BF16), w2_ref[...])
        t = -wl
        sp = jnp.maximum(t, 0.0) + jnp.log1p(jnp.exp(-jnp.abs(t)))
        put('lw', rows, -jnp.exp(-sp - 0.5))
        yield
        put('a', rows, _sigmoid(vec[1:2] + _dot(_dot(m[4], a1_ref[...]).astype(BF16), a2_ref[...])))
        yield
        put('gate', rows, _dot(_sigmoid(_dot(m[5], g1_ref[...])).astype(BF16), g2_ref[...]).astype(BF16))
        yield
        if nxt is not None:
            last, m = nxt
    carry_ref[...] = last
    return last


def _rwkv_proj_kernel(x_ref, g_ref, mu_ref, wrkv_ref, vec_ref, w1_ref, w2_ref, a1_ref, a2_ref,
                      g1_ref, g2_ref, sh0_ref,
                      r_ref, k_ref, v_ref, lw_ref, a_ref, gate_ref, sh_ref, carry_ref, *, sub):
    @pl.when(pl.program_id(1) == 0)
    def _():
        carry_ref[...] = sh0_ref[0]

    outs = dict(r=r_ref, k=k_ref, v=v_ref, lw=lw_ref, a=a_ref, gate=gate_ref)

    def put(name, rows, val):
        outs[name][rows, :] = val

    sh_ref[0] = _run(_proj_stages(x_ref, g_ref, mu_ref, wrkv_ref, vec_ref, w1_ref, w2_ref, a1_ref, a2_ref,
                                  g1_ref, g2_ref, carry_ref, put, sub=sub))


def _rwkv_proj(x, norm_g, i, W, j, sh0, *, nseq, seq, tm, sub):
    M, D = x.shape
    npt = seq // tm
    row = lambda b, t: (b * npt + t, 0)
    st = lambda b, t: (b, 0, 0)
    big = pl.BlockSpec((tm, D), row)
    sd = lambda dt: jax.ShapeDtypeStruct((M, D), dt)
    names = ('b_mu', 'b_w_rkv', 'b_vec', 'b_w1', 'b_w2', 'b_a1', 'b_a2', 'b_g1', 'b_g2')
    return pl.pallas_call(
        functools.partial(_rwkv_proj_kernel, sub=sub),
        grid=(nseq, npt),
        in_specs=[big, _resident(norm_g, i)] + [_resident(W[n], j) for n in names]
                 + [pl.BlockSpec((1, 1, D), st)],
        out_specs=[big, big, big, big, big, big, pl.BlockSpec((1, 1, D), st)],
        out_shape=[sd(BF16), sd(F32), sd(BF16), sd(F32), sd(F32), sd(BF16),
                   jax.ShapeDtypeStruct((nseq, 1, D), F32)],
        scratch_shapes=[pltpu.VMEM((1, D), F32)],
        compiler_params=_cparams("arbitrary", "arbitrary"),
        name="rwkv_proj",
    )(x, norm_g, *[W[n] for n in names], sh0)


def _wkv_chunk(r, k, v, lw, a, gate, prm):
    C = r[0].shape[0]
    C2 = 2 * C
    H = RWKV_HEAD
    pm = lambda f, *ls: [f(*xs) for xs in zip(*ls)]
    bf = lambda xs: [x.astype(BF16) for x in xs]
    lane = lax.broadcasted_iota(jnp.int32, (1, LANES), 1)
    low = lane < H
    hr = lax.broadcasted_iota(jnp.int32, (LANES, LANES), 0)
    hc = lax.broadcasted_iota(jnp.int32, (LANES, LANES), 1)
    head_bd = (hr < H) == (hc < H)
    ones_bd = jnp.where(head_bd, 1.0, 0.0).astype(BF16)
    fold = lambda x: x[:C] + x[C:]

    def segsum(xs):
        zs = [jnp.zeros_like(x) for x in xs]
        s0 = [jnp.sum(jnp.where(low, x, z), axis=-1, keepdims=True) for x, z in zip(xs, zs)]
        s1 = [jnp.sum(jnp.where(low, z, x), axis=-1, keepdims=True) for x, z in zip(xs, zs)]
        return [jnp.where(low, a_, b_) for a_, b_ in zip(s0, s1)]

    rowi = lax.broadcasted_iota(jnp.int32, (C, 1), 0)

    def cumsum_rows(x):
        s = 1
        while s < C:
            x = x + jnp.where(rowi >= s, pltpu.roll(x, s, axis=0), 0.0)
            s *= 2
        return x

    def stack(x):
        z = jnp.zeros_like(x)
        return jnp.concatenate([jnp.where(low, x, z), jnp.where(low, z, x)], axis=0)

    def prep():
        cum = [cumsum_rows(x) for x in lw]
        kkr = pm(lambda k_, p_: k_ * p_[0:1], k, prm)
        ss = segsum([x * x for x in kkr])
        yield
        G = [jnp.exp(c) for c in cum]
        Gm1 = pm(lambda c, l: jnp.exp(c - l), cum, lw)
        iG = [jnp.exp(-c) for c in cum]
        G_end = [jnp.exp(c[C - 1:C, :]) for c in cum]
        G_rest = [jnp.exp(c[C - 1:C, :] - c) for c in cum]
        kk = pm(lambda x, s_: x * lax.rsqrt(jnp.maximum(s_, 1e-24)), kkr, ss)
        k2 = pm(lambda k_, a_, p_: k_ * (1.0 + (a_ - 1.0) * p_[1:2]), k, a, prm)
        b = pm(lambda x, a_: x * a_, kk, a)
        At = pm(lambda x, g_: -x * g_, kk, Gm1)
        Rt = pm(lambda x, g_: x * g_, r, G)

        tt = lax.broadcasted_iota(jnp.int32, (C, C2), 0)
        ts = lax.broadcasted_iota(jnp.int32, (C, C2), 1) & (C - 1)
        strict = tt > ts
        incl = tt >= ts
        first = lax.broadcasted_iota(jnp.int32, (1, C2), 1) < C

        def bdiag(x):
            z = jnp.zeros_like(x)
            return jnp.concatenate([jnp.where(first, x, z), jnp.where(first, z, x)], axis=0).astype(BF16)

        lhs_s = pm(lambda x, y: jnp.concatenate([x, y], axis=0).astype(BF16), At, Rt)
        Bt = pm(lambda x, g_: stack(x * g_).astype(BF16), b, iG)
        Kt = pm(lambda x, g_: stack(x * g_).astype(BF16), k2, iG)
        P = pm(lambda l_, b_, k_: _dot_nt(l_, jnp.concatenate([b_, k_], axis=0)), lhs_s, Bt, Kt)
        rk = segsum(pm(lambda r_, k_, p_: r_ * k_ * p_[2:3], r, k2, prm))
        yield
        Lab = [jnp.where(strict, x[:C, :C2], 0.0) for x in P]
        Lak = [jnp.where(strict, x[:C, C2:], 0.0).astype(BF16) for x in P]
        Lrr = [jnp.concatenate([jnp.where(incl, x[C:, :C2], 0.0), jnp.where(incl, x[C:, C2:], 0.0)],
                               axis=1).astype(BF16) for x in P]
        Vst = [stack(x).astype(BF16) for x in v]
        LV = pm(_dot, Lak, Vst)

        blk = lambda s: (tt >> int(math.log2(s))) == (ts >> int(math.log2(s)))
        eye = jnp.where(tt == ts, 1.0, 0.0)
        L1 = [jnp.where(blk(8), x, 0.0) for x in Lab]
        L2 = pm(lambda x: _dot(x.astype(BF16), bdiag(x)), L1)
        yield
        L4 = pm(lambda x: _dot(x.astype(BF16), bdiag(x)), L2)
        T = pm(lambda x, y: _dot((eye + x).astype(BF16), bdiag(eye + y)), L1, L2)
        yield
        T = pm(lambda x, y: _dot(x.astype(BF16), bdiag(eye + y)), T, L4)
        yield
        s = 8
        while s < C:
            msk = blk(2 * s) & jnp.logical_not(blk(s))
            upper = [(o + s, o + 2 * s) for o in range(0, C, 2 * s)]
            Mx = [bdiag(jnp.where(msk, x, 0.0)) for x in Lab]
            Tu = [jnp.concatenate([t[a_:b_] for a_, b_ in upper], axis=0).astype(BF16) for t in T]
            TM = pm(_dot, Tu, Mx)
            yield
            X = pm(lambda tm_, t: _dot(tm_.astype(BF16), bdiag(t)), TM, T)
            yield
            T = [jnp.concatenate([piece for n, (a_, b_) in enumerate(upper)
                                  for piece in (t[a_ - s:a_], t[a_:b_] + x[n * s:(n + 1) * s])], axis=0)
                 for t, x in zip(T, X)]
            s *= 2
        rhs_s = pm(lambda b_, k_, g_: jnp.concatenate([b_ * g_, k_ * g_], axis=0).astype(BF16), b, k2, G_rest)
        return dict(Tb=bf(T), lhs_s=lhs_s, rhs_s=rhs_s, LV=LV, Lrr=Lrr, Vst=Vst, G_end=G_end, rk=rk)

    def apply(q, S):
        AZ = pm(lambda l_, s_: _dot_nt(l_, s_.astype(BF16)), q['lhs_s'], S)
        yield
        U = pm(lambda t_, az, lv: _dot(t_, stack(az[:C] + lv).astype(BF16)), q['Tb'], AZ, q['LV'])
        yield
        Y = pm(lambda az, l_, u_, v_: az[C:] + _dot(l_, jnp.concatenate([stack(u_).astype(BF16), v_], axis=0)),
               AZ, q['Lrr'], U, q['Vst'])
        upd = pm(lambda u_, v_, rhs_: _dot_tn(jnp.concatenate([u_, v_], axis=0).astype(BF16), rhs_),
                 U, v, q['rhs_s'])
        yield
        S_new = pm(lambda s_, g_, u_: s_ * g_ + jnp.where(head_bd, u_, 0.0), S, q['G_end'], upd)
        mean = [x * (1.0 / H) for x in segsum(Y)]
        yield
        d = pm(lambda y_, m_: y_ - m_, Y, mean)
        var = [x * (1.0 / H) for x in segsum([x * x for x in d])]
        yield
        out = pm(lambda d_, var_, p_, rk_, v_, g_:
                 (d_ * lax.rsqrt(var_ + GN_EPS) * p_[3:4] + p_[4:5] + rk_ * v_) * g_,
                 d, var, prm, q['rk'], v, gate)
        return out, S_new

    return prep, apply


PREP_AHEAD = 2


def _wkv_stages(load, prm_ref, sf_ref, z_ref, *, tb, C):
    n_pairs = z_ref.shape[1] // LANES
    n_chunks = tb // C
    lanes = [slice(p * LANES, (p + 1) * LANES) for p in range(n_pairs)]
    prm = [prm_ref[:, ln] for ln in lanes]

    def stages(c):
        rows = slice(c * C, (c + 1) * C)
        ld = lambda name: [load(name, rows, ln).astype(F32) for ln in lanes]
        return _wkv_chunk(*[ld(name) for name, _ in PROJ_OUT], prm)

    S = [sf_ref[0, p] for p in range(n_pairs)]
    applies, ready, in_flight = {}, {}, []
    launched = done = 0
    cur = None
    while done < n_chunks:
        while len(in_flight) < PREP_AHEAD and launched < n_chunks:
            prep, applies[launched] = stages(launched)
            in_flight.append((launched, prep()))
            launched += 1
        if cur is None and done in ready:
            cur = applies.pop(done)(ready.pop(done), S)
        if cur is not None:
            finished, res = _step(cur)
            if finished:
                z, S = res
                for p, ln in enumerate(lanes):
                    z_ref[done * C:(done + 1) * C, ln] = z[p].astype(BF16)
                cur = None
                done += 1
        for c, gen in list(in_flight):
            finished, res = _step(gen)
            if finished:
                ready[c] = res
                in_flight.remove((c, gen))
        yield
    for p in range(n_pairs):
        sf_ref[0, p] = S[p]


def _wkv_kernel(r_ref, k_ref, v_ref, lw_ref, a_ref, g_ref, prm_ref, s0_ref, z_ref, sf_ref, *, C):
    @pl.when(pl.program_id(1) == 0)
    def _():
        sf_ref[...] = s0_ref[...]

    refs = dict(r=r_ref, k=k_ref, v=v_ref, lw=lw_ref, a=a_ref, gate=g_ref)
    _run(_wkv_stages(lambda name, rows, ln: refs[name][rows, ln], prm_ref, sf_ref, z_ref,
                     tb=r_ref.shape[0], C=C))


def _wkv(r, k, v, lw, a, gate, prm, j, s0, *, nseq, seq, tb, C):
    M, D = r.shape
    npt = seq // tb
    row = lambda b, t: (b * npt + t, 0)
    big = pl.BlockSpec((tb, D), row)
    st = pl.BlockSpec((1,) + s0.shape[1:], lambda b, t: (b, 0, 0, 0))
    return pl.pallas_call(
        functools.partial(_wkv_kernel, C=C),
        grid=(nseq, npt),
        in_specs=[big, big, big, big, big, big, _resident(prm, j), st],
        out_specs=[big, st],
        out_shape=[jax.ShapeDtypeStruct((M, D), BF16), jax.ShapeDtypeStruct(s0.shape, F32)],
        compiler_params=_cparams("arbitrary", "arbitrary"),
        name="rwkv_wkv",
    )(r, k, v, lw, a, gate, prm, s0)


def _pair_state(s):
    B, Hh, N, _ = s.shape
    s = s.reshape(B, Hh // 2, 2, N, N)
    z = jnp.zeros((B, Hh // 2, N, N), s.dtype)
    return jnp.concatenate([jnp.concatenate([s[:, :, 0], z], axis=-1),
                            jnp.concatenate([z, s[:, :, 1]], axis=-1)], axis=-2)


def _unpair_state(z):
    N = z.shape[-1] // 2
    return jnp.stack([z[:, :, :N, :N], z[:, :, N:, N:]], axis=2).reshape(z.shape[0], -1, N, N)


TILE_ROWS = dict(post=512, seq=512, rwkv_proj=512, rwkv_sub=256, wkv=512)


def _tile(n, pref):
    t = min(n, pref)
    assert n % t == 0
    return t


def _trunk(x3, prompt, a_k, a_v, b_wkv, b_shift, c_conv, W):
    B, T, D = x3.shape
    M = B * T
    x = x3.reshape(M, D)
    norm_g = W['norm_g']
    n_heads = D // HEAD_DIM
    n_kv = W['n_kv']
    kvw = n_kv * LANES
    nk, nv, nwkv, nsh, ncv = [], [], [], [], []
    seq_tile = _tile(T, TILE_ROWS['seq'])
    for i in range(norm_g.shape[0]):
        kind, j = i % N_MIXERS, i // N_MIXERS
        if kind == 0:
            q, kv, tail = _qkv(x, norm_g, i, W['a_w_qkv'], j, seq=T, tm=seq_tile)
            tail_k = tail[:, :, :kvw].reshape(B, -1, n_kv, 2, HEAD_DIM)[:, :, :, 0]
            tail_v = tail[:, :, kvw:].reshape(B, -1, n_kv, 2, HEAD_DIM)[:, :, :, 0]
            if prompt:
                nk.append(tail_k)
                nv.append(tail_v)
                x = _attn_post(q, kv, x, W['rel_bias_table'], W['a_sinks'][j], W['a_w_o'], j, norm_g,
                               W['mlp_w1'], W['mlp_w2'], i, seq=T, tq=seq_tile, qc=CHUNK)
                continue
            ck, cv = a_k[j], a_v[j]
            prev = jnp.concatenate([_dup_heads(ck.reshape(B, WINDOW, -1), n_kv),
                                    _dup_heads(cv.reshape(B, WINDOW, -1), n_kv)],
                                   axis=-1).astype(BF16).reshape(B * WINDOW, 2 * kvw)
            z = _attn(q, kv, prev, lambda b, t: (b, 0), W['rel_bias_table'], W['a_sinks'][j], nseq=B, seq=T,
                      tq=seq_tile, qc=T, masked=False)
            nk.append(jnp.concatenate([ck, tail_k], axis=1)[:, -WINDOW:])
            nv.append(jnp.concatenate([cv, tail_v], axis=1)[:, -WINDOW:])
            w_out = W['a_w_o']
        elif kind == 1:
            if prompt:
                sh0 = jnp.zeros((B, 1, D), F32)
                s0 = jnp.zeros((B, n_heads // 2, LANES, LANES), F32)
            else:
                sh0 = b_shift[j].reshape(B, 1, D)
                s0 = _pair_state(b_wkv[j])
            tm = _tile(T, TILE_ROWS['rwkv_proj'])
            r, k, v, lw, a, gate, sh_new = _rwkv_proj(x, norm_g, i, W, j, sh0, nseq=B, seq=T, tm=tm,
                                                      sub=_tile(tm, TILE_ROWS['rwkv_sub']))
            z, s_fin = _wkv(r, k, v, lw, a, gate, W['b_prm'], j, s0, nseq=B, seq=T,
                            tb=_tile(T, TILE_ROWS['wkv']), C=min(T, CHUNK))
            nwkv.append(_unpair_state(s_fin))
            nsh.append(sh_new.reshape(B, D))
            w_out = W['b_w_o']
        else:
            u0 = jnp.zeros((B, CONV_WIDTH - 1, D), F32) if prompt else c_conv[j]
            z, u_new = _conv(x, norm_g, i, W['c_w_in'], W['c_conv_w'], j, u0, nseq=B, seq=T, tm=seq_tile)
            ncv.append(u_new)
            w_out = W['c_w_out']
        x = _post(x, z, w_out, j, norm_g, W['mlp_w1'], W['mlp_w2'], i, tm=_tile(M, TILE_ROWS['post']))
    return (x.reshape(B, T, D), jnp.stack(nk), jnp.stack(nv), jnp.stack(nwkv), jnp.stack(nsh),
            jnp.stack(ncv))


def kernel(x_prompt, x_sample, cache_a_k, cache_a_v, state_b_wkv, state_b_shift, state_c_conv, rel_bias_table, norm_g, a_w_qkv, a_w_o, a_sinks, b_mu, b_w_rkv, b_w_o, b_w0, b_w1, b_w2, b_a0, b_a1, b_a2, b_g1, b_g2, b_k_k, b_k_a, b_r_k, b_ln_w, b_ln_b, c_w_in, c_conv_w, c_w_out, mlp_w1, mlp_w2):
    D = x_prompt.shape[-1]
    n_kv = cache_a_k.shape[3]
    nq = D
    nkv = n_kv * HEAD_DIM
    bf = lambda t: t.astype(BF16)
    wq, wk, wv = a_w_qkv[..., :nq], a_w_qkv[..., nq:nq + nkv], a_w_qkv[..., nq + nkv:]
    w_qkv = bf(jnp.concatenate([wq * (HEAD_DIM ** -0.5 * LOG2E), _dup_heads(wk, n_kv), _dup_heads(wv, n_kv)],
                               axis=-1))
    n_b = b_mu.shape[0]
    zeros = jnp.zeros_like(b_w0)
    W = dict(
        n_kv=n_kv, rel_bias_table=rel_bias_table, norm_g=norm_g, a_w_qkv=w_qkv, a_w_o=bf(a_w_o), a_sinks=a_sinks,
        b_mu=b_mu, b_w_rkv=bf(b_w_rkv), b_w_o=bf(b_w_o),
        b_vec=jnp.stack([b_w0, b_a0], axis=1),
        b_w1=bf(b_w1), b_w2=bf(b_w2), b_a1=bf(b_a1), b_a2=bf(b_a2), b_g1=bf(b_g1), b_g2=bf(b_g2),
        b_prm=jnp.stack([b_k_k, b_k_a, b_r_k.reshape(n_b, D), b_ln_w, b_ln_b, zeros, zeros, zeros], axis=1),
        c_w_in=bf(c_w_in), c_conv_w=c_conv_w, c_w_out=bf(c_w_out), mlp_w1=bf(mlp_w1), mlp_w2=bf(mlp_w2))
    y_p, ak_p, av_p, wkv_p, sh_p, cv_p = _trunk(x_prompt, True, None, None, None, None, None, W)
    y_s, ak_s, av_s, wkv_s, sh_s, cv_s = _trunk(x_sample, False, cache_a_k, cache_a_v, state_b_wkv,
                                                state_b_shift, state_c_conv, W)
    return (y_p, y_s, ak_p, av_p, ak_s, av_s, wkv_p, wkv_s, sh_p, sh_s, cv_p, cv_s)
```

```python
import functools
import math

import jax
import jax.numpy as jnp
from jax import lax
from jax.experimental import pallas as pl
from jax.experimental.pallas import tpu as pltpu

F32 = jnp.float32
BF16 = jnp.bfloat16

HEAD_DIM = 64
CHUNK = 64
WINDOW = 128
NUM_BUCKETS = 32
MAX_DISTANCE = 128
RWKV_HEAD = 64
CONV_WIDTH = 3
N_MIXERS = 3
RMS_EPS = 1e-6
GN_EPS = RWKV_HEAD * 1e-5
LOG2E = math.log2(math.e)

LANES = 128
VMEM_LIMIT = 56 * 1024 * 1024


def _cparams(*sem):
    return pltpu.CompilerParams(dimension_semantics=sem, vmem_limit_bytes=VMEM_LIMIT)


def _resident(arr, layer=None):
    if layer is None:
        nd = arr.ndim
        return pl.BlockSpec(arr.shape, lambda *_: (0,) * nd, pipeline_mode=pl.Buffered(1))
    nd = arr.ndim - 1
    return pl.BlockSpec((None,) + arr.shape[1:], lambda *_: (layer,) + (0,) * nd, pipeline_mode=pl.Buffered(1))


def _rms(x, g):
    return x * lax.rsqrt(jnp.mean(x * x, axis=-1, keepdims=True) + RMS_EPS) * g


def _dot(a, b):
    return jnp.dot(a, b, preferred_element_type=F32)


def _dot_nt(a, b):
    return lax.dot_general(a, b, (((1,), (1,)), ((), ())), preferred_element_type=F32)


def _dot_tn(a, b):
    return lax.dot_general(a, b, (((0,), (0,)), ((), ())), preferred_element_type=F32)


def _sigmoid(x):
    return 1.0 / (1.0 + jnp.exp(-x))


def _step(gen):
    try:
        next(gen)
        return False, None
    except StopIteration as stop:
        return True, stop.value


def _run(*gens, strides=None):
    strides = strides or [1] * len(gens)
    results = [None] * len(gens)
    live = list(range(len(gens)))
    while live:
        for n in list(live):
            for _ in range(strides[n]):
                finished, value = _step(gens[n])
                if finished:
                    results[n] = value
                    live.remove(n)
                    break
    return results[0] if len(gens) == 1 else results


def _post_stages(x_ref, load_z, wo_ref, g_ref, w1_ref, w2_ref, o_ref, acc_ref, *, ff_chunk):
    g = g_ref[...]
    m = _dot(load_z(), wo_ref[...])
    x1 = x_ref[...] + _rms(m, g[1:2])
    h2 = _rms(x1, g[2:3]).astype(BF16)
    yield
    d_ff = w1_ref.shape[1]
    for c in range(d_ff // ff_chunk):
        sl = slice(c * ff_chunk, (c + 1) * ff_chunk)
        a = jnp.maximum(_dot(h2, w1_ref[:, sl]), 0.0)
        a = (a * a).astype(BF16)
        yield
        part = _dot(a, w2_ref[sl, :])
        if c == 0:
            acc_ref[...] = part
        else:
            acc_ref[...] += part
        yield
    o_ref[...] = x1 + _rms(acc_ref[...], g[3:4])


def _post_kernel(x_ref, z_ref, wo_ref, g_ref, w1_ref, w2_ref, o_ref, acc_ref, *, ff_chunk):
    _run(_post_stages(x_ref, lambda: z_ref[...], wo_ref, g_ref, w1_ref, w2_ref, o_ref, acc_ref,
                      ff_chunk=ff_chunk))


def _post(x, z, w_out, j, norm_g, w1, w2, i, *, tm):
    M, D = x.shape
    d_ff = w1.shape[2]
    row = lambda t: (t, 0)
    return pl.pallas_call(
        functools.partial(_post_kernel, ff_chunk=min(d_ff, 1024)),
        grid=(M // tm,),
        in_specs=[pl.BlockSpec((tm, D), row), pl.BlockSpec((tm, D), row),
                  _resident(w_out, j), _resident(norm_g, i), _resident(w1, i), _resident(w2, i)],
        out_specs=pl.BlockSpec((tm, D), row),
        out_shape=jax.ShapeDtypeStruct((M, D), F32),
        scratch_shapes=[pltpu.VMEM((tm, D), F32)],
        compiler_params=_cparams("parallel"),
        name="post_mlp",
    )(x, z, w_out, norm_g, w1, w2)


def _qkv_kernel(x_ref, g_ref, w_ref, q_ref, kv_ref, tail_ref, *, tiles_per_seq, tail):
    h = _rms(x_ref[...], g_ref[0:1, :]).astype(BF16)
    p = _dot(h, w_ref[...])
    D = q_ref.shape[1]
    q_ref[...] = p[:, :D].astype(BF16)
    kv_ref[...] = p[:, D:].astype(BF16)
    tm = p.shape[0]

    @pl.when(pl.program_id(0) % tiles_per_seq == tiles_per_seq - 1)
    def _():
        tail_ref[0] = p[tm - tail:, D:]


def _qkv(x, norm_g, i, w, j, *, seq, tm):
    M, D = x.shape
    KV = w.shape[2] - D
    tail = min(WINDOW, seq)
    assert seq % tm == 0 and tail <= tm
    tps = seq // tm
    row = lambda i: (i, 0)
    return pl.pallas_call(
        functools.partial(_qkv_kernel, tiles_per_seq=tps, tail=tail),
        grid=(M // tm,),
        in_specs=[pl.BlockSpec((tm, D), row), _resident(norm_g, i), _resident(w, j)],
        out_specs=[pl.BlockSpec((tm, D), row), pl.BlockSpec((tm, KV), row),
                   pl.BlockSpec((1, tail, KV), lambda t: (t // tps, 0, 0))],
        out_shape=[jax.ShapeDtypeStruct((M, D), BF16), jax.ShapeDtypeStruct((M, KV), BF16),
                   jax.ShapeDtypeStruct((M // seq, tail, KV), F32)],
        compiler_params=_cparams("arbitrary"),
        name="attn_qkv",
    )(x, norm_g, w)


def _head_order(n_heads, n_kv):
    group = n_heads // n_kv
    return tuple(group * (2 * g + half) + i for g in range(n_kv // 2) for i in range(group) for half in (0, 1))


def _attn_kernel(sink_ref, table_ref, bucket_ref, q_ref, prev_ref, cur_ref, o_ref, kvx_ref, bias_ref, *,
                 qc, masked):
    @pl.when((pl.program_id(0) == 0) & (pl.program_id(1) == 0))
    def _():
        _attn_bias_init(table_ref, bucket_ref, bias_ref, cur_ref.shape[1] // 2 // HEAD_DIM, qc=qc, masked=masked)

    def put(rows, lanes, val):
        o_ref[rows, lanes] = val

    _run(_attn_stages(sink_ref, q_ref, prev_ref, cur_ref, kvx_ref, bias_ref, put, qc=qc,
                      first_chunk=pl.program_id(1) * (q_ref.shape[0] // qc)))


def _attn_bias_init(table_ref, bucket_ref, bias_ref, n_kv, *, qc, masked):
    n_var, n_pairs, _, band = bias_ref.shape
    keypos = lax.broadcasted_iota(jnp.int32, (1, band), 1)
    bucket = bucket_ref[...]
    for slot, h in enumerate(_head_order(2 * n_pairs, n_kv)):
        pick = lambda b, acc: jnp.where(bucket == b, table_ref[b, h], acc)
        bias_h = lax.fori_loop(0, table_ref.shape[0], pick, jnp.zeros((qc, band), F32)) * LOG2E
        for v in range(n_var):
            hidden = keypos < (WINDOW - v * qc if masked else 0)
            bias_ref[v, slot // 2, (slot % 2) * qc:(slot % 2 + 1) * qc, :] = jnp.where(hidden, -jnp.inf, bias_h)


def _attn_stages(sink_ref, q_ref, prev_ref, cur_ref, kvx_ref, bias_ref, put, *, qc, first_chunk):
    tq, D = q_ref.shape
    kvw = cur_ref.shape[1] // 2
    band = WINDOW + qc
    n_var = bias_ref.shape[0]
    n_pairs = D // LANES
    pairs_per_kv = n_pairs // (kvw // LANES)
    order = _head_order(2 * n_pairs, kvw // HEAD_DIM)
    kvx_ref[0:WINDOW, :] = prev_ref[...]
    kvx_ref[WINDOW:, :] = cur_ref[...]
    lane = lax.broadcasted_iota(jnp.int32, (1, LANES), 1)
    low = lane < HEAD_DIM
    row2 = lax.broadcasted_iota(jnp.int32, (2 * qc, 1), 0)
    pairs = range(n_pairs)
    sink = [jnp.where(row2 < qc, sink_ref[order[2 * p]], sink_ref[order[2 * p + 1]]) * LOG2E for p in pairs]

    for c in range(tq // qc):
        qrows = q_ref[c * qc:(c + 1) * qc, :]
        kvb = kvx_ref[c * qc:c * qc + band, :]
        var = jnp.minimum(first_chunk + c, n_var - 1)

        def scores(p):
            qp = qrows[:, p * LANES:(p + 1) * LANES]
            zero = jnp.zeros_like(qp)
            lhs = jnp.concatenate([jnp.where(low, qp, zero), jnp.where(low, zero, qp)], axis=0)
            hk = p // pairs_per_kv
            return _dot_nt(lhs, kvb[:, hk * LANES:(hk + 1) * LANES]) + bias_ref[var, p]

        s = [scores(p) for p in pairs]
        m = [jnp.maximum(jnp.max(s[p], axis=-1, keepdims=True), sink[p]) for p in pairs]
        e = [jnp.exp2(s[p] - m[p]) for p in pairs]
        den = [jnp.sum(e[p], axis=-1, keepdims=True) + jnp.exp2(sink[p] - m[p]) for p in pairs]
        yield
        o2 = [_dot(e[p].astype(BF16),
                   kvb[:, kvw + (p // pairs_per_kv) * LANES: kvw + (p // pairs_per_kv + 1) * LANES]) for p in pairs]
        for p in pairs:
            o = o2[p] / den[p]
            put(slice(c * qc, (c + 1) * qc), slice(p * LANES, (p + 1) * LANES),
                jnp.where(low, o[:qc], o[qc:]).astype(BF16))
        yield


def _bucket_map(qc):
    qi = jnp.arange(qc)[:, None]
    kj = jnp.arange(WINDOW + qc)[None, :]
    return _rel_bucket(kj - WINDOW - qi)


def _attn(q, kv, prev, prev_map, table, sinks, *, nseq, seq, tq, qc, masked):
    M, D = q.shape
    KV2 = kv.shape[1]
    npt = seq // tq
    n_var = WINDOW // qc + 1 if masked else 1
    bucket = _bucket_map(qc)
    row = lambda b, t: (b * npt + t, 0)
    smem = pl.BlockSpec(memory_space=pltpu.SMEM)
    return pl.pallas_call(
        functools.partial(_attn_kernel, qc=qc, masked=masked),
        grid=(nseq, npt),
        in_specs=[smem, smem, _resident(bucket),
                  pl.BlockSpec((tq, D), row),
                  pl.BlockSpec((WINDOW, KV2), prev_map),
                  pl.BlockSpec((tq, KV2), row)],
        out_specs=pl.BlockSpec((tq, D), row),
        out_shape=jax.ShapeDtypeStruct((M, D), BF16),
        scratch_shapes=[pltpu.VMEM((WINDOW + tq, KV2), BF16),
                        pltpu.VMEM((n_var, D // LANES, 2 * qc, WINDOW + qc), F32)],
        compiler_params=_cparams("arbitrary", "arbitrary"),
        name="attn_core",
    )(sinks, table, bucket, q, prev, kv)


def _attn_post_kernel(sink_ref, table_ref, bucket_ref, q_ref, prev_ref, cur_ref, x_ref, wo_ref, g_ref, w1_ref,
                      w2_ref, out_ref, kvx_ref, bias_ref, o_scr, acc_ref, *, qc, tiles_per_seq, n_tiles, ff_chunk):
    t = pl.program_id(0)

    @pl.when(t == 0)
    def _():
        _attn_bias_init(table_ref, bucket_ref, bias_ref, cur_ref.shape[1] // 2 // HEAD_DIM, qc=qc, masked=True)
        o_scr[...] = jnp.zeros_like(o_scr)

    tile = jnp.minimum(t, n_tiles - 1)
    slot_w = t % 2
    slot_r = (t + 1) % 2

    def put(rows, lanes, val):
        o_scr[slot_w, rows, lanes] = val

    attn = _attn_stages(sink_ref, q_ref, prev_ref, cur_ref, kvx_ref, bias_ref, put, qc=qc,
                        first_chunk=(tile % tiles_per_seq) * (q_ref.shape[0] // qc))
    post = _post_stages(x_ref, lambda: o_scr[slot_r], wo_ref, g_ref, w1_ref, w2_ref, out_ref, acc_ref,
                        ff_chunk=ff_chunk)
    _run(post, attn)


def _attn_post(q, kv, x, table, sinks, w_out, j, norm_g, w1, w2, i, *, seq, tq, qc):
    M, D = q.shape
    KV2 = kv.shape[1]
    d_ff = w1.shape[2]
    n_tiles = M // tq
    n_var = WINDOW // qc + 1
    bucket = _bucket_map(qc)
    cur = lambda t: (jnp.minimum(t, n_tiles - 1), 0)
    prev = lambda t: (jnp.maximum(jnp.minimum(t, n_tiles - 1) * (tq // WINDOW) - 1, 0), 0)
    lag = lambda t: (jnp.maximum(t - 1, 0), 0)
    smem = pl.BlockSpec(memory_space=pltpu.SMEM)
    return pl.pallas_call(
        functools.partial(_attn_post_kernel, qc=qc, tiles_per_seq=seq // tq, n_tiles=n_tiles,
                          ff_chunk=min(d_ff, 512)),
        grid=(n_tiles + 1,),
        in_specs=[smem, smem, _resident(bucket),
                  pl.BlockSpec((tq, D), cur), pl.BlockSpec((WINDOW, KV2), prev), pl.BlockSpec((tq, KV2), cur),
                  pl.BlockSpec((tq, D), lag),
                  _resident(w_out, j), _resident(norm_g, i), _resident(w1, i), _resident(w2, i)],
        out_specs=pl.BlockSpec((tq, D), lag),
        out_shape=jax.ShapeDtypeStruct((M, D), F32),
        scratch_shapes=[pltpu.VMEM((WINDOW + tq, KV2), BF16),
                        pltpu.VMEM((n_var, D // LANES, 2 * qc, WINDOW + qc), F32),
                        pltpu.VMEM((2, tq, D), BF16),
                        pltpu.VMEM((tq, D), F32)],
        compiler_params=_cparams("arbitrary"),
        name="attn_post_mlp",
    )(sinks, table, bucket, q, kv, kv, x, w_out, norm_g, w1, w2)


def _rel_bucket(rp):
    nb = NUM_BUCKETS // 2
    ret = (rp > 0).astype(jnp.int32) * nb
    n = jnp.abs(rp)
    max_exact = nb // 2
    nf = jnp.maximum(n, 1).astype(F32)
    large = max_exact + (jnp.log(nf / max_exact) / math.log(MAX_DISTANCE / max_exact)
                         * (nb - max_exact)).astype(jnp.int32)
    large = jnp.minimum(large, nb - 1)
    return ret + jnp.where(n < max_exact, n, large)


def _conv_kernel(x_ref, g_ref, w_ref, cw_ref, u0_ref, z_ref, un_ref, carry_ref):
    @pl.when(pl.program_id(1) == 0)
    def _():
        carry_ref[...] = u0_ref[0]

    D = x_ref.shape[1]
    h = _rms(x_ref[...], g_ref[0:1, :]).astype(BF16)
    p = _dot(h, w_ref[...])
    bg, u = p[:, :D], p[:, D:2 * D] * p[:, 2 * D:]
    tm = u.shape[0]
    row = lax.broadcasted_iota(jnp.int32, (tm, 1), 0)
    c0, c1 = carry_ref[0:1, :], carry_ref[1:2, :]
    um1 = jnp.where(row == 0, c1, pltpu.roll(u, 1, axis=0))
    um2 = jnp.where(row == 0, c0, jnp.where(row == 1, c1, pltpu.roll(u, 2, axis=0)))
    cw = cw_ref[...]
    y = um2 * cw[0:1] + um1 * cw[1:2] + u * cw[2:3]
    z_ref[...] = (bg * y).astype(BF16)
    carry_ref[...] = u[tm - 2:, :]
    un_ref[0] = u[tm - 2:, :]


def _conv(x, norm_g, i, w_in, cw, j, u0, *, nseq, seq, tm):
    M, D = x.shape
    npt = seq // tm
    row = lambda b, t: (b * npt + t, 0)
    st = lambda b, t: (b, 0, 0)
    return pl.pallas_call(
        _conv_kernel,
        grid=(nseq, npt),
        in_specs=[pl.BlockSpec((tm, D), row), _resident(norm_g, i), _resident(w_in, j),
                  _resident(cw, j), pl.BlockSpec((1, CONV_WIDTH - 1, D), st)],
        out_specs=[pl.BlockSpec((tm, D), row), pl.BlockSpec((1, CONV_WIDTH - 1, D), st)],
        out_shape=[jax.ShapeDtypeStruct((M, D), BF16),
                   jax.ShapeDtypeStruct((nseq, CONV_WIDTH - 1, D), F32)],
        scratch_shapes=[pltpu.VMEM((CONV_WIDTH - 1, D), F32)],
        compiler_params=_cparams("arbitrary", "arbitrary"),
        name="conv_mix",
    )(x, norm_g, w_in, cw, u0)


PROJ_OUT = (('r', BF16), ('k', F32), ('v', BF16), ('lw', F32), ('a', F32), ('gate', BF16))


def _proj_stages(x_ref, g_ref, mu_ref, wrkv_ref, vec_ref, w1_ref, w2_ref, a1_ref, a2_ref, g1_ref, g2_ref,
                 carry_ref, put, *, sub):
    tm = x_ref.shape[0]
    mu = mu_ref[...]
    vec = vec_ref[...]
    row = lax.broadcasted_iota(jnp.int32, (sub, 1), 0)

    def mixes(s, prev_last):
        h = _rms(x_ref[s * sub:(s + 1) * sub, :], g_ref[0:1, :])
        xx = jnp.where(row == 0, prev_last, pltpu.roll(h, 1, axis=0)) - h
        return h[sub - 1:, :], [(h + xx * mu[i:i + 1]).astype(BF16) for i in range(6)]

    last, m = mixes(0, carry_ref[...])
    yield
    for s in range(tm // sub):
        rows = slice(s * sub, (s + 1) * sub)
        nxt = mixes(s + 1, last) if (s + 1) * sub < tm else None
        put('r', rows, _dot(m[0], wrkv_ref[0]).astype(BF16))
        yield
        put('k', rows, _dot(m[2], wrkv_ref[1]))
        yield
        put('v', rows, _dot(m[3], wrkv_ref[2]).astype(BF16))
        yield
        wl = vec[0:1] + _dot(jnp.tanh(_dot(m[1], w1_ref[...])).astype(BF16), w2_ref[...])
        t = -wl
        sp = jnp.maximum(t, 0.0) + jnp.log(1.0 + jnp.exp(-jnp.abs(t)))
        put('lw', rows, -jnp.exp(-sp - 0.5))
        yield
        put('a', rows, _sigmoid(vec[1:2] + _dot(_dot(m[4], a1_ref[...]).astype(BF16), a2_ref[...])))
        yield
        put('gate', rows, _dot(_sigmoid(_dot(m[5], g1_ref[...])).astype(BF16), g2_ref[...]).astype(BF16))
        yield
        if nxt is not None:
            last, m = nxt
    carry_ref[...] = last
    return last


def _rwkv_proj_kernel(x_ref, g_ref, mu_ref, wrkv_ref, vec_ref, w1_ref, w2_ref, a1_ref, a2_ref,
                      g1_ref, g2_ref, sh0_ref,
                      r_ref, k_ref, v_ref, lw_ref, a_ref, gate_ref, sh_ref, carry_ref, *, sub):
    @pl.when(pl.program_id(1) == 0)
    def _():
        carry_ref[...] = sh0_ref[0]

    outs = dict(r=r_ref, k=k_ref, v=v_ref, lw=lw_ref, a=a_ref, gate=gate_ref)

    def put(name, rows, val):
        outs[name][rows, :] = val

    sh_ref[0] = _run(_proj_stages(x_ref, g_ref, mu_ref, wrkv_ref, vec_ref, w1_ref, w2_ref, a1_ref, a2_ref,
                                  g1_ref, g2_ref, carry_ref, put, sub=sub))


def _rwkv_proj(x, norm_g, i, W, j, sh0, *, nseq, seq, tm, sub):
    M, D = x.shape
    npt = seq // tm
    row = lambda b, t: (b * npt + t, 0)
    st = lambda b, t: (b, 0, 0)
    big = pl.BlockSpec((tm, D), row)
    sd = lambda dt: jax.ShapeDtypeStruct((M, D), dt)
    names = ('b_mu', 'b_w_rkv', 'b_vec', 'b_w1', 'b_w2', 'b_a1', 'b_a2', 'b_g1', 'b_g2')
    return pl.pallas_call(
        functools.partial(_rwkv_proj_kernel, sub=sub),
        grid=(nseq, npt),
        in_specs=[big, _resident(norm_g, i)] + [_resident(W[n], j) for n in names]
                 + [pl.BlockSpec((1, 1, D), st)],
        out_specs=[big, big, big, big, big, big, pl.BlockSpec((1, 1, D), st)],
        out_shape=[sd(BF16), sd(F32), sd(BF16), sd(F32), sd(F32), sd(BF16),
                   jax.ShapeDtypeStruct((nseq, 1, D), F32)],
        scratch_shapes=[pltpu.VMEM((1, D), F32)],
        compiler_params=_cparams("arbitrary", "arbitrary"),
        name="rwkv_proj",
    )(x, norm_g, *[W[n] for n in names], sh0)


def _wkv_chunk(r, k, v, lw, a, gate, prm):
    C = r[0].shape[0]
    C2 = 2 * C
    H = RWKV_HEAD
    pm = lambda f, *ls: [f(*xs) for xs in zip(*ls)]
    bf = lambda xs: [x.astype(BF16) for x in xs]
    lane = lax.broadcasted_iota(jnp.int32, (1, LANES), 1)
    low = lane < H
    hr = lax.broadcasted_iota(jnp.int32, (LANES, LANES), 0)
    hc = lax.broadcasted_iota(jnp.int32, (LANES, LANES), 1)
    head_bd = (hr < H) == (hc < H)
    ones_bd = jnp.where(head_bd, 1.0, 0.0).astype(BF16)
    fold = lambda x: x[:C] + x[C:]

    def segsum(xs):
        zs = [jnp.zeros_like(x) for x in xs]
        s0 = [jnp.sum(jnp.where(low, x, z), axis=-1, keepdims=True) for x, z in zip(xs, zs)]
        s1 = [jnp.sum(jnp.where(low, z, x), axis=-1, keepdims=True) for x, z in zip(xs, zs)]
        return [jnp.where(low, a_, b_) for a_, b_ in zip(s0, s1)]

    rowi = lax.broadcasted_iota(jnp.int32, (C, 1), 0)

    def cumsum_rows(x):
        s = 1
        while s < C:
            x = x + jnp.where(rowi >= s, pltpu.roll(x, s, axis=0), 0.0)
            s *= 2
        return x

    def stack(x):
        z = jnp.zeros_like(x)
        return jnp.concatenate([jnp.where(low, x, z), jnp.where(low, z, x)], axis=0)

    def prep():
        cum = [cumsum_rows(x) for x in lw]
        kkr = pm(lambda k_, p_: k_ * p_[0:1], k, prm)
        ss = segsum([x * x for x in kkr])
        yield
        G = [jnp.exp(c) for c in cum]
        Gm1 = pm(lambda c, l: jnp.exp(c - l), cum, lw)
        iG = [jnp.exp(-c) for c in cum]
        G_end = [jnp.exp(c[C - 1:C, :]) for c in cum]
        G_rest = [jnp.exp(c[C - 1:C, :] - c) for c in cum]
        kk = pm(lambda x, s_: x * lax.rsqrt(jnp.maximum(s_, 1e-24)), kkr, ss)
        k2 = pm(lambda k_, a_, p_: k_ * (1.0 + (a_ - 1.0) * p_[1:2]), k, a, prm)
        b = pm(lambda x, a_: x * a_, kk, a)
        At = pm(lambda x, g_: -x * g_, kk, Gm1)
        Rt = pm(lambda x, g_: x * g_, r, G)

        tt = lax.broadcasted_iota(jnp.int32, (C, C2), 0)
        ts = lax.broadcasted_iota(jnp.int32, (C, C2), 1) & (C - 1)
        strict = tt > ts
        incl = tt >= ts
        first = lax.broadcasted_iota(jnp.int32, (1, C2), 1) < C

        def bdiag(x):
            z = jnp.zeros_like(x)
            return jnp.concatenate([jnp.where(first, x, z), jnp.where(first, z, x)], axis=0).astype(BF16)

        lhs_s = pm(lambda x, y: jnp.concatenate([x, y], axis=0).astype(BF16), At, Rt)
        Bt = pm(lambda x, g_: stack(x * g_).astype(BF16), b, iG)
        Kt = pm(lambda x, g_: stack(x * g_).astype(BF16), k2, iG)
        P = pm(lambda l_, b_, k_: _dot_nt(l_, jnp.concatenate([b_, k_], axis=0)), lhs_s, Bt, Kt)
        rk = segsum(pm(lambda r_, k_, p_: r_ * k_ * p_[2:3], r, k2, prm))
        yield
        Lab = [jnp.where(strict, x[:C, :C2], 0.0) for x in P]
        Lak = [jnp.where(strict, x[:C, C2:], 0.0).astype(BF16) for x in P]
        Lrr = [jnp.concatenate([jnp.where(incl, x[C:, :C2], 0.0), jnp.where(incl, x[C:, C2:], 0.0)],
                               axis=1).astype(BF16) for x in P]
        Vst = [stack(x).astype(BF16) for x in v]
        LV = pm(_dot, Lak, Vst)

        blk = lambda s: (tt >> int(math.log2(s))) == (ts >> int(math.log2(s)))
        eye = jnp.where(tt == ts, 1.0, 0.0)
        L1 = [jnp.where(blk(8), x, 0.0) for x in Lab]
        L2 = pm(lambda x: _dot(x.astype(BF16), bdiag(x)), L1)
        yield
        L4 = pm(lambda x: _dot(x.astype(BF16), bdiag(x)), L2)
        T = pm(lambda x, y: _dot((eye + x).astype(BF16), bdiag(eye + y)), L1, L2)
        yield
        T = pm(lambda x, y: _dot(x.astype(BF16), bdiag(eye + y)), T, L4)
        yield
        s = 8
        while s < C:
            msk = blk(2 * s) & jnp.logical_not(blk(s))
            upper = [(o + s, o + 2 * s) for o in range(0, C, 2 * s)]
            Mx = [bdiag(jnp.where(msk, x, 0.0)) for x in Lab]
            Tu = [jnp.concatenate([t[a_:b_] for a_, b_ in upper], axis=0).astype(BF16) for t in T]
            TM = pm(_dot, Tu, Mx)
            yield
            X = pm(lambda tm_, t: _dot(tm_.astype(BF16), bdiag(t)), TM, T)
            yield
            T = [jnp.concatenate([piece for n, (a_, b_) in enumerate(upper)
                                  for piece in (t[a_ - s:a_], t[a_:b_] + x[n * s:(n + 1) * s])], axis=0)
                 for t, x in zip(T, X)]
            s *= 2
        rhs_s = pm(lambda b_, k_, g_: jnp.concatenate([b_ * g_, k_ * g_], axis=0).astype(BF16), b, k2, G_rest)
        return dict(Tb=bf(T), lhs_s=lhs_s, rhs_s=rhs_s, LV=LV, Lrr=Lrr, Vst=Vst, G_end=G_end, rk=rk)

    def apply(q, S):
        AZ = pm(lambda l_, s_: _dot_nt(l_, s_.astype(BF16)), q['lhs_s'], S)
        yield
        U = pm(lambda t_, az, lv: _dot(t_, stack(az[:C] + lv).astype(BF16)), q['Tb'], AZ, q['LV'])
        yield
        Y = pm(lambda az, l_, u_, v_: az[C:] + _dot(l_, jnp.concatenate([stack(u_).astype(BF16), v_], axis=0)),
               AZ, q['Lrr'], U, q['Vst'])
        upd = pm(lambda u_, v_, rhs_: _dot_tn(jnp.concatenate([u_, v_], axis=0).astype(BF16), rhs_),
                 U, v, q['rhs_s'])
        yield
        S_new = pm(lambda s_, g_, u_: s_ * g_ + jnp.where(head_bd, u_, 0.0), S, q['G_end'], upd)
        mean = [x * (1.0 / H) for x in segsum(Y)]
        yield
        d = pm(lambda y_, m_: y_ - m_, Y, mean)
        var = [x * (1.0 / H) for x in segsum([x * x for x in d])]
        yield
        out = pm(lambda d_, var_, p_, rk_, v_, g_:
                 (d_ * lax.rsqrt(var_ + GN_EPS) * p_[3:4] + p_[4:5] + rk_ * v_) * g_,
                 d, var, prm, q['rk'], v, gate)
        return out, S_new

    return prep, apply


PREP_AHEAD = 2


def _wkv_stages(load, prm_ref, sf_ref, put_z, *, tb, C):
    n_pairs = prm_ref.shape[1] // LANES
    n_chunks = tb // C
    lanes = [slice(p * LANES, (p + 1) * LANES) for p in range(n_pairs)]
    prm = [prm_ref[:, ln] for ln in lanes]

    def stages(c):
        rows = slice(c * C, (c + 1) * C)
        ld = lambda name: [load(name, rows, ln).astype(F32) for ln in lanes]
        return _wkv_chunk(*[ld(name) for name, _ in PROJ_OUT], prm)

    S = [sf_ref[0, p] for p in range(n_pairs)]
    applies, ready, in_flight = {}, {}, []
    launched = done = 0
    cur = None
    while done < n_chunks:
        while len(in_flight) < PREP_AHEAD and launched < n_chunks:
            prep, applies[launched] = stages(launched)
            in_flight.append((launched, prep()))
            launched += 1
        if cur is None and done in ready:
            cur = applies.pop(done)(ready.pop(done), S)
        if cur is not None:
            finished, res = _step(cur)
            if finished:
                z, S = res
                for p, ln in enumerate(lanes):
                    put_z(slice(done * C, (done + 1) * C), ln, z[p].astype(BF16))
                cur = None
                done += 1
        for c, gen in list(in_flight):
            finished, res = _step(gen)
            if finished:
                ready[c] = res
                in_flight.remove((c, gen))
        yield
    for p in range(n_pairs):
        sf_ref[0, p] = S[p]


def _wkv_kernel(r_ref, k_ref, v_ref, lw_ref, a_ref, g_ref, prm_ref, s0_ref, z_ref, sf_ref, *, C):
    @pl.when(pl.program_id(1) == 0)
    def _():
        sf_ref[...] = s0_ref[...]

    refs = dict(r=r_ref, k=k_ref, v=v_ref, lw=lw_ref, a=a_ref, gate=g_ref)

    def put_z(rows, ln, val):
        z_ref[rows, ln] = val

    _run(_wkv_stages(lambda name, rows, ln: refs[name][rows, ln], prm_ref, sf_ref, put_z,
                     tb=r_ref.shape[0], C=C))


def _wkv(r, k, v, lw, a, gate, prm, j, s0, *, nseq, seq, tb, C):
    M, D = r.shape
    npt = seq // tb
    row = lambda b, t: (b * npt + t, 0)
    big = pl.BlockSpec((tb, D), row)
    st = pl.BlockSpec((1,) + s0.shape[1:], lambda b, t: (b, 0, 0, 0))
    return pl.pallas_call(
        functools.partial(_wkv_kernel, C=C),
        grid=(nseq, npt),
        in_specs=[big, big, big, big, big, big, _resident(prm, j), st],
        out_specs=[big, st],
        out_shape=[jax.ShapeDtypeStruct((M, D), BF16), jax.ShapeDtypeStruct(s0.shape, F32)],
        compiler_params=_cparams("arbitrary", "arbitrary"),
        name="rwkv_wkv",
    )(r, k, v, lw, a, gate, prm, s0)


def _pair_state(s):
    B, Hh, N, _ = s.shape
    s = s.reshape(B, Hh // 2, 2, N, N)
    z = jnp.zeros((B, Hh // 2, N, N), s.dtype)
    return jnp.concatenate([jnp.concatenate([s[:, :, 0], z], axis=-1),
                            jnp.concatenate([z, s[:, :, 1]], axis=-1)], axis=-2)


def _unpair_state(z):
    N = z.shape[-1] // 2
    return jnp.stack([z[:, :, :N, :N], z[:, :, N:, N:]], axis=2).reshape(z.shape[0], -1, N, N)


TILE_ROWS = dict(post=512, seq=512, rwkv_proj=512, rwkv_sub=256, wkv=512)


def _tile(n, pref):
    t = min(n, pref)
    assert n % t == 0
    return t


def _trunk(x3, prompt, a_k, a_v, b_wkv, b_shift, c_conv, W):
    B, T, D = x3.shape
    M = B * T
    x = x3.reshape(M, D)
    norm_g = W['norm_g']
    n_heads = D // HEAD_DIM
    n_kv = W['n_kv']
    kvw = n_kv * HEAD_DIM
    nk, nv, nwkv, nsh, ncv = [], [], [], [], []
    seq_tile = _tile(T, TILE_ROWS['seq'])
    for i in range(norm_g.shape[0]):
        kind, j = i % N_MIXERS, i // N_MIXERS
        if kind == 0:
            q, kv, tail = _qkv(x, norm_g, i, W['a_w_qkv'], j, seq=T, tm=seq_tile)
            tail_k = tail[:, :, :kvw].reshape(B, -1, n_kv, HEAD_DIM)
            tail_v = tail[:, :, kvw:].reshape(B, -1, n_kv, HEAD_DIM)
            if prompt:
                nk.append(tail_k)
                nv.append(tail_v)
                x = _attn_post(q, kv, x, W['rel_bias_table'], W['a_sinks'][j], W['a_w_o'], j, norm_g,
                               W['mlp_w1'], W['mlp_w2'], i, seq=T, tq=seq_tile, qc=CHUNK)
                continue
            ck, cv = a_k[j], a_v[j]
            prev = jnp.concatenate([ck.reshape(B * WINDOW, kvw), cv.reshape(B * WINDOW, kvw)], axis=-1).astype(BF16)
            z = _attn(q, kv, prev, lambda b, t: (b, 0), W['rel_bias_table'], W['a_sinks'][j], nseq=B, seq=T,
                      tq=seq_tile, qc=T, masked=False)
            nk.append(jnp.concatenate([ck, tail_k], axis=1)[:, -WINDOW:])
            nv.append(jnp.concatenate([cv, tail_v], axis=1)[:, -WINDOW:])
            w_out = W['a_w_o']
        elif kind == 1:
            if prompt:
                sh0 = jnp.zeros((B, 1, D), F32)
                s0 = jnp.zeros((B, n_heads // 2, LANES, LANES), F32)
            else:
                sh0 = b_shift[j].reshape(B, 1, D)
                s0 = _pair_state(b_wkv[j])
            tm = _tile(T, TILE_ROWS['rwkv_proj'])
            r, k, v, lw, a, gate, sh_new = _rwkv_proj(x, norm_g, i, W, j, sh0, nseq=B, seq=T, tm=tm,
                                                      sub=_tile(tm, TILE_ROWS['rwkv_sub']))
            nsh.append(sh_new.reshape(B, D))
            z, s_fin = _wkv(r, k, v, lw, a, gate, W['b_prm'], j, s0, nseq=B, seq=T,
                            tb=_tile(T, TILE_ROWS['wkv']), C=min(T, CHUNK))
            nwkv.append(_unpair_state(s_fin))
            w_out = W['b_w_o']
        else:
            u0 = jnp.zeros((B, CONV_WIDTH - 1, D), F32) if prompt else c_conv[j]
            z, u_new = _conv(x, norm_g, i, W['c_w_in'], W['c_conv_w'], j, u0, nseq=B, seq=T, tm=seq_tile)
            ncv.append(u_new)
            w_out = W['c_w_out']
        x = _post(x, z, w_out, j, norm_g, W['mlp_w1'], W['mlp_w2'], i, tm=_tile(M, TILE_ROWS['post']))
    return (x.reshape(B, T, D), jnp.stack(nk), jnp.stack(nv), jnp.stack(nwkv), jnp.stack(nsh),
            jnp.stack(ncv))


def kernel(x_prompt, x_sample, cache_a_k, cache_a_v, state_b_wkv, state_b_shift, state_c_conv, rel_bias_table, norm_g, a_w_qkv, a_w_o, a_sinks, b_mu, b_w_rkv, b_w_o, b_w0, b_w1, b_w2, b_a0, b_a1, b_a2, b_g1, b_g2, b_k_k, b_k_a, b_r_k, b_ln_w, b_ln_b, c_w_in, c_conv_w, c_w_out, mlp_w1, mlp_w2):
    D = x_prompt.shape[-1]
    n_kv = cache_a_k.shape[3]
    nq = D
    bf = lambda t: t.astype(BF16)
    order = jnp.array(_head_order(nq // HEAD_DIM, n_kv))
    n_a = a_w_qkv.shape[0]
    wq = (a_w_qkv[..., :nq] * (HEAD_DIM ** -0.5 * LOG2E)).reshape(n_a, D, -1, HEAD_DIM)[:, :, order]
    w_qkv = bf(jnp.concatenate([wq.reshape(n_a, D, nq), a_w_qkv[..., nq:]], axis=-1))
    a_w_o = a_w_o.reshape(n_a, -1, HEAD_DIM, D)[:, order].reshape(a_w_o.shape)
    n_b = b_mu.shape[0]
    zeros = jnp.zeros_like(b_w0)
    W = dict(
        n_kv=n_kv, rel_bias_table=rel_bias_table, norm_g=norm_g, a_w_qkv=w_qkv, a_w_o=bf(a_w_o), a_sinks=a_sinks,
        b_mu=b_mu, b_w_rkv=bf(b_w_rkv), b_w_o=bf(b_w_o),
        b_vec=jnp.stack([b_w0, b_a0], axis=1),
        b_w1=bf(b_w1), b_w2=bf(b_w2), b_a1=bf(b_a1), b_a2=bf(b_a2), b_g1=bf(b_g1), b_g2=bf(b_g2),
        b_prm=jnp.stack([b_k_k, b_k_a, b_r_k.reshape(n_b, D), b_ln_w, b_ln_b, zeros, zeros, zeros], axis=1),
        c_w_in=bf(c_w_in), c_conv_w=c_conv_w, c_w_out=bf(c_w_out), mlp_w1=bf(mlp_w1), mlp_w2=bf(mlp_w2))
    y_p, ak_p, av_p, wkv_p, sh_p, cv_p = _trunk(x_prompt, True, None, None, None, None, None, W)
    y_s, ak_s, av_s, wkv_s, sh_s, cv_s = _trunk(x_sample, False, cache_a_k, cache_a_v, state_b_wkv,
                                                state_b_shift, state_c_conv, W)
    return (y_p, y_s, ak_p, av_p, ak_s, av_s, wkv_p, wkv_s, sh_p, sh_s, cv_p, cv_s)
```

```python
import functools
import math

import jax
import jax.numpy as jnp
from jax import lax
from jax.experimental import pallas as pl
from jax.experimental.pallas import tpu as pltpu

F32 = jnp.float32
BF16 = jnp.bfloat16

HEAD_DIM = 64
CHUNK = 64
WINDOW = 128
NUM_BUCKETS = 32
MAX_DISTANCE = 128
RWKV_HEAD = 64
CONV_WIDTH = 3
N_MIXERS = 3
RMS_EPS = 1e-6
GN_EPS = RWKV_HEAD * 1e-5
LOG2E = math.log2(math.e)

LANES = 128
VMEM_LIMIT = 56 * 1024 * 1024


def _cparams(*sem):
    return pltpu.CompilerParams(dimension_semantics=sem, vmem_limit_bytes=VMEM_LIMIT)


def _resident(arr, layer=None):
    if layer is None:
        nd = arr.ndim
        return pl.BlockSpec(arr.shape, lambda *_: (0,) * nd, pipeline_mode=pl.Buffered(1))
    nd = arr.ndim - 1
    return pl.BlockSpec((None,) + arr.shape[1:], lambda *_: (layer,) + (0,) * nd, pipeline_mode=pl.Buffered(1))


def _rms(x, g):
    return x * lax.rsqrt(jnp.mean(x * x, axis=-1, keepdims=True) + RMS_EPS) * g


def _dot(a, b):
    return jnp.dot(a, b, preferred_element_type=F32)


def _dot_nt(a, b):
    return lax.dot_general(a, b, (((1,), (1,)), ((), ())), preferred_element_type=F32)


def _dot_tn(a, b):
    return lax.dot_general(a, b, (((0,), (0,)), ((), ())), preferred_element_type=F32)


def _sigmoid(x):
    return 1.0 / (1.0 + jnp.exp(-x))


def _step(gen):
    try:
        next(gen)
        return False, None
    except StopIteration as stop:
        return True, stop.value


def _run(*gens, strides=None):
    strides = strides or [1] * len(gens)
    results = [None] * len(gens)
    live = list(range(len(gens)))
    while live:
        for n in list(live):
            for _ in range(strides[n]):
                finished, value = _step(gens[n])
                if finished:
                    results[n] = value
                    live.remove(n)
                    break
    return results[0] if len(gens) == 1 else results


def _post_stages(x_ref, load_z, wo_ref, g_ref, w1_ref, w2_ref, o_ref, acc_ref, *, ff_chunk):
    g = g_ref[...]
    m = _dot(load_z(), wo_ref[...])
    x1 = x_ref[...] + _rms(m, g[1:2])
    h2 = _rms(x1, g[2:3]).astype(BF16)
    yield
    d_ff = w1_ref.shape[1]
    for c in range(d_ff // ff_chunk):
        sl = slice(c * ff_chunk, (c + 1) * ff_chunk)
        a = jnp.maximum(_dot(h2, w1_ref[:, sl]), 0.0)
        a = (a * a).astype(BF16)
        yield
        part = _dot(a, w2_ref[sl, :])
        if c == 0:
            acc_ref[...] = part
        else:
            acc_ref[...] += part
        yield
    o_ref[...] = x1 + _rms(acc_ref[...], g[3:4])


def _post_kernel(x_ref, z_ref, wo_ref, g_ref, w1_ref, w2_ref, o_ref, acc_ref, *, ff_chunk):
    _run(_post_stages(x_ref, lambda: z_ref[...], wo_ref, g_ref, w1_ref, w2_ref, o_ref, acc_ref,
                      ff_chunk=ff_chunk))


def _post(x, z, w_out, j, norm_g, w1, w2, i, *, tm):
    M, D = x.shape
    d_ff = w1.shape[2]
    row = lambda t: (t, 0)
    return pl.pallas_call(
        functools.partial(_post_kernel, ff_chunk=min(d_ff, 1024)),
        grid=(M // tm,),
        in_specs=[pl.BlockSpec((tm, D), row), pl.BlockSpec((tm, D), row),
                  _resident(w_out, j), _resident(norm_g, i), _resident(w1, i), _resident(w2, i)],
        out_specs=pl.BlockSpec((tm, D), row),
        out_shape=jax.ShapeDtypeStruct((M, D), F32),
        scratch_shapes=[pltpu.VMEM((tm, D), F32)],
        compiler_params=_cparams("parallel"),
        name="post_mlp",
    )(x, z, w_out, norm_g, w1, w2)


def _qkv_kernel(x_ref, g_ref, w_ref, q_ref, kv_ref, tail_ref, *, tiles_per_seq, tail):
    h = _rms(x_ref[...], g_ref[0:1, :]).astype(BF16)
    p = _dot(h, w_ref[...])
    D = q_ref.shape[1]
    q_ref[...] = p[:, :D].astype(BF16)
    kv_ref[...] = p[:, D:].astype(BF16)
    tm = p.shape[0]

    @pl.when(pl.program_id(0) % tiles_per_seq == tiles_per_seq - 1)
    def _():
        tail_ref[0] = p[tm - tail:, D:]


def _qkv(x, norm_g, i, w, j, *, seq, tm):
    M, D = x.shape
    KV = w.shape[2] - D
    tail = min(WINDOW, seq)
    assert seq % tm == 0 and tail <= tm
    tps = seq // tm
    row = lambda i: (i, 0)
    return pl.pallas_call(
        functools.partial(_qkv_kernel, tiles_per_seq=tps, tail=tail),
        grid=(M // tm,),
        in_specs=[pl.BlockSpec((tm, D), row), _resident(norm_g, i), _resident(w, j)],
        out_specs=[pl.BlockSpec((tm, D), row), pl.BlockSpec((tm, KV), row),
                   pl.BlockSpec((1, tail, KV), lambda t: (t // tps, 0, 0))],
        out_shape=[jax.ShapeDtypeStruct((M, D), BF16), jax.ShapeDtypeStruct((M, KV), BF16),
                   jax.ShapeDtypeStruct((M // seq, tail, KV), F32)],
        compiler_params=_cparams("arbitrary"),
        name="attn_qkv",
    )(x, norm_g, w)


def _head_order(n_heads, n_kv):
    group = n_heads // n_kv
    return tuple(group * (2 * g + half) + i for g in range(n_kv // 2) for i in range(group) for half in (0, 1))


def _attn_kernel(sink_ref, table_ref, bucket_ref, q_ref, prev_ref, cur_ref, o_ref, kvx_ref, bias_ref, *,
                 qc, masked):
    @pl.when((pl.program_id(0) == 0) & (pl.program_id(1) == 0))
    def _():
        _attn_bias_init(table_ref, bucket_ref, bias_ref, cur_ref.shape[1] // 2 // HEAD_DIM, qc=qc, masked=masked)

    def put(rows, lanes, val):
        o_ref[rows, lanes] = val

    _run(_attn_stages(sink_ref, q_ref, prev_ref, cur_ref, kvx_ref, bias_ref, put, qc=qc,
                      first_chunk=pl.program_id(1) * (q_ref.shape[0] // qc)))


def _attn_bias_init(table_ref, bucket_ref, bias_ref, n_kv, *, qc, masked):
    n_var, n_pairs, _, band = bias_ref.shape
    keypos = lax.broadcasted_iota(jnp.int32, (1, band), 1)
    bucket = bucket_ref[...]
    for slot, h in enumerate(_head_order(2 * n_pairs, n_kv)):
        pick = lambda b, acc: jnp.where(bucket == b, table_ref[b, h], acc)
        bias_h = lax.fori_loop(0, table_ref.shape[0], pick, jnp.zeros((qc, band), F32)) * LOG2E
        for v in range(n_var):
            hidden = keypos < (WINDOW - v * qc if masked else 0)
            bias_ref[v, slot // 2, (slot % 2) * qc:(slot % 2 + 1) * qc, :] = jnp.where(hidden, -jnp.inf, bias_h)


def _attn_stages(sink_ref, q_ref, prev_ref, cur_ref, kvx_ref, bias_ref, put, *, qc, first_chunk):
    tq, D = q_ref.shape
    kvw = cur_ref.shape[1] // 2
    band = WINDOW + qc
    n_var = bias_ref.shape[0]
    n_pairs = D // LANES
    pairs_per_kv = n_pairs // (kvw // LANES)
    order = _head_order(2 * n_pairs, kvw // HEAD_DIM)
    kvx_ref[0:WINDOW, :] = prev_ref[...]
    kvx_ref[WINDOW:, :] = cur_ref[...]
    lane = lax.broadcasted_iota(jnp.int32, (1, LANES), 1)
    low = lane < HEAD_DIM
    row2 = lax.broadcasted_iota(jnp.int32, (2 * qc, 1), 0)
    pairs = range(n_pairs)
    sink = [jnp.where(row2 < qc, sink_ref[order[2 * p]], sink_ref[order[2 * p + 1]]) * LOG2E for p in pairs]

    for c in range(tq // qc):
        qrows = q_ref[c * qc:(c + 1) * qc, :]
        kvb = kvx_ref[c * qc:c * qc + band, :]
        var = jnp.minimum(first_chunk + c, n_var - 1)

        def scores(p):
            qp = qrows[:, p * LANES:(p + 1) * LANES]
            zero = jnp.zeros_like(qp)
            lhs = jnp.concatenate([jnp.where(low, qp, zero), jnp.where(low, zero, qp)], axis=0)
            hk = p // pairs_per_kv
            return _dot_nt(lhs, kvb[:, hk * LANES:(hk + 1) * LANES]) + bias_ref[var, p]

        s = [scores(p) for p in pairs]
        m = [jnp.maximum(jnp.max(s[p], axis=-1, keepdims=True), sink[p]) for p in pairs]
        e = [jnp.exp2(s[p] - m[p]) for p in pairs]
        den = [jnp.sum(e[p], axis=-1, keepdims=True) + jnp.exp2(sink[p] - m[p]) for p in pairs]
        yield
        o2 = [_dot(e[p].astype(BF16),
                   kvb[:, kvw + (p // pairs_per_kv) * LANES: kvw + (p // pairs_per_kv + 1) * LANES]) for p in pairs]
        for p in pairs:
            o = o2[p] / den[p]
            put(slice(c * qc, (c + 1) * qc), slice(p * LANES, (p + 1) * LANES),
                jnp.where(low, o[:qc], o[qc:]).astype(BF16))
        yield


def _bucket_map(qc):
    qi = jnp.arange(qc)[:, None]
    kj = jnp.arange(WINDOW + qc)[None, :]
    return _rel_bucket(kj - WINDOW - qi)


def _attn(q, kv, prev, prev_map, table, sinks, *, nseq, seq, tq, qc, masked):
    M, D = q.shape
    KV2 = kv.shape[1]
    npt = seq // tq
    n_var = WINDOW // qc + 1 if masked else 1
    bucket = _bucket_map(qc)
    row = lambda b, t: (b * npt + t, 0)
    smem = pl.BlockSpec(memory_space=pltpu.SMEM)
    return pl.pallas_call(
        functools.partial(_attn_kernel, qc=qc, masked=masked),
        grid=(nseq, npt),
        in_specs=[smem, smem, _resident(bucket),
                  pl.BlockSpec((tq, D), row),
                  pl.BlockSpec((WINDOW, KV2), prev_map),
                  pl.BlockSpec((tq, KV2), row)],
        out_specs=pl.BlockSpec((tq, D), row),
        out_shape=jax.ShapeDtypeStruct((M, D), BF16),
        scratch_shapes=[pltpu.VMEM((WINDOW + tq, KV2), BF16),
                        pltpu.VMEM((n_var, D // LANES, 2 * qc, WINDOW + qc), F32)],
        compiler_params=_cparams("arbitrary", "arbitrary"),
        name="attn_core",
    )(sinks, table, bucket, q, prev, kv)


def _attn_post_kernel(sink_ref, table_ref, bucket_ref, q_ref, prev_ref, cur_ref, x_ref, wo_ref, g_ref, w1_ref,
                      w2_ref, out_ref, kvx_ref, bias_ref, o_scr, acc_ref, *, qc, tiles_per_seq, n_tiles, ff_chunk):
    t = pl.program_id(0)

    @pl.when(t == 0)
    def _():
        _attn_bias_init(table_ref, bucket_ref, bias_ref, cur_ref.shape[1] // 2 // HEAD_DIM, qc=qc, masked=True)
        o_scr[...] = jnp.zeros_like(o_scr)

    tile = jnp.minimum(t, n_tiles - 1)
    slot_w = t % 2
    slot_r = (t + 1) % 2

    def put(rows, lanes, val):
        o_scr[slot_w, rows, lanes] = val

    attn = _attn_stages(sink_ref, q_ref, prev_ref, cur_ref, kvx_ref, bias_ref, put, qc=qc,
                        first_chunk=(tile % tiles_per_seq) * (q_ref.shape[0] // qc))
    post = _post_stages(x_ref, lambda: o_scr[slot_r], wo_ref, g_ref, w1_ref, w2_ref, out_ref, acc_ref,
                        ff_chunk=ff_chunk)
    _run(post, attn)


def _attn_post(q, kv, x, table, sinks, w_out, j, norm_g, w1, w2, i, *, seq, tq, qc):
    M, D = q.shape
    KV2 = kv.shape[1]
    d_ff = w1.shape[2]
    n_tiles = M // tq
    n_var = WINDOW // qc + 1
    bucket = _bucket_map(qc)
    cur = lambda t: (jnp.minimum(t, n_tiles - 1), 0)
    prev = lambda t: (jnp.maximum(jnp.minimum(t, n_tiles - 1) * (tq // WINDOW) - 1, 0), 0)
    lag = lambda t: (jnp.maximum(t - 1, 0), 0)
    smem = pl.BlockSpec(memory_space=pltpu.SMEM)
    return pl.pallas_call(
        functools.partial(_attn_post_kernel, qc=qc, tiles_per_seq=seq // tq, n_tiles=n_tiles,
                          ff_chunk=min(d_ff, 512)),
        grid=(n_tiles + 1,),
        in_specs=[smem, smem, _resident(bucket),
                  pl.BlockSpec((tq, D), cur), pl.BlockSpec((WINDOW, KV2), prev), pl.BlockSpec((tq, KV2), cur),
                  pl.BlockSpec((tq, D), lag),
                  _resident(w_out, j), _resident(norm_g, i), _resident(w1, i), _resident(w2, i)],
        out_specs=pl.BlockSpec((tq, D), lag),
        out_shape=jax.ShapeDtypeStruct((M, D), F32),
        scratch_shapes=[pltpu.VMEM((WINDOW + tq, KV2), BF16),
                        pltpu.VMEM((n_var, D // LANES, 2 * qc, WINDOW + qc), F32),
                        pltpu.VMEM((2, tq, D), BF16),
                        pltpu.VMEM((tq, D), F32)],
        compiler_params=_cparams("arbitrary"),
        name="attn_post_mlp",
    )(sinks, table, bucket, q, kv, kv, x, w_out, norm_g, w1, w2)


def _rel_bucket(rp):
    nb = NUM_BUCKETS // 2
    ret = (rp > 0).astype(jnp.int32) * nb
    n = jnp.abs(rp)
    max_exact = nb // 2
    nf = jnp.maximum(n, 1).astype(F32)
    large = max_exact + (jnp.log(nf / max_exact) / math.log(MAX_DISTANCE / max_exact)
                         * (nb - max_exact)).astype(jnp.int32)
    large = jnp.minimum(large, nb - 1)
    return ret + jnp.where(n < max_exact, n, large)


CONV_COLS = 256
PROJ_COLS = 256


def _conv_kernel(x_ref, g_ref, w_ref, cw_ref, u0_ref, z_ref, un_ref, carry_ref):
    @pl.when(pl.program_id(1) == 0)
    def _():
        carry_ref[...] = u0_ref[0]

    tm, D = x_ref.shape
    h = _rms(x_ref[...], g_ref[0:1, :]).astype(BF16)
    row = lax.broadcasted_iota(jnp.int32, (tm, 1), 0)
    for c in range(0, D, CONV_COLS):
        cols = slice(c, c + CONV_COLS)
        bg = _dot(h, w_ref[:, c:c + CONV_COLS])
        u = _dot(h, w_ref[:, D + c:D + c + CONV_COLS]) * _dot(h, w_ref[:, 2 * D + c:2 * D + c + CONV_COLS])
        c0, c1 = carry_ref[0:1, cols], carry_ref[1:2, cols]
        um1 = jnp.where(row == 0, c1, pltpu.roll(u, 1, axis=0))
        um2 = jnp.where(row == 0, c0, jnp.where(row == 1, c1, pltpu.roll(u, 2, axis=0)))
        cw = cw_ref[:, cols]
        y = um2 * cw[0:1] + um1 * cw[1:2] + u * cw[2:3]
        z_ref[:, cols] = (bg * y).astype(BF16)
        carry_ref[:, cols] = u[tm - 2:, :]
        un_ref[0, :, cols] = u[tm - 2:, :]


def _conv(x, norm_g, i, w_in, cw, j, u0, *, nseq, seq, tm):
    M, D = x.shape
    npt = seq // tm
    row = lambda b, t: (b * npt + t, 0)
    st = lambda b, t: (b, 0, 0)
    return pl.pallas_call(
        _conv_kernel,
        grid=(nseq, npt),
        in_specs=[pl.BlockSpec((tm, D), row), _resident(norm_g, i), _resident(w_in, j),
                  _resident(cw, j), pl.BlockSpec((1, CONV_WIDTH - 1, D), st)],
        out_specs=[pl.BlockSpec((tm, D), row), pl.BlockSpec((1, CONV_WIDTH - 1, D), st)],
        out_shape=[jax.ShapeDtypeStruct((M, D), BF16),
                   jax.ShapeDtypeStruct((nseq, CONV_WIDTH - 1, D), F32)],
        scratch_shapes=[pltpu.VMEM((CONV_WIDTH - 1, D), F32)],
        compiler_params=_cparams("arbitrary", "arbitrary"),
        name="conv_mix",
    )(x, norm_g, w_in, cw, u0)


PROJ_OUT = (('r', BF16), ('k', F32), ('v', BF16), ('lw', F32), ('a', F32), ('gate', BF16))


def _proj_stages(x_ref, g_ref, mu_ref, wrkv_ref, vec_ref, w1_ref, w2_ref, a1_ref, a2_ref, g1_ref, g2_ref,
                 carry_ref, put, *, sub):
    tm = x_ref.shape[0]
    mu = mu_ref[...]
    vec = vec_ref[...]
    row = lax.broadcasted_iota(jnp.int32, (sub, 1), 0)

    def mixes(s, prev_last):
        h = _rms(x_ref[s * sub:(s + 1) * sub, :], g_ref[0:1, :])
        xx = jnp.where(row == 0, prev_last, pltpu.roll(h, 1, axis=0)) - h
        return h[sub - 1:, :], [(h + xx * mu[i:i + 1]).astype(BF16) for i in range(6)]

    last, m = mixes(0, carry_ref[...])
    yield
    for s in range(tm // sub):
        rows = slice(s * sub, (s + 1) * sub)
        nxt = mixes(s + 1, last) if (s + 1) * sub < tm else None
        w_mid = jnp.tanh(_dot(m[1], w1_ref[...])).astype(BF16)
        a_mid = _dot(m[4], a1_ref[...]).astype(BF16)
        g_mid = _sigmoid(_dot(m[5], g1_ref[...])).astype(BF16)
        yield
        for c0 in range(0, wrkv_ref.shape[2], PROJ_COLS):
            cols = slice(c0, c0 + PROJ_COLS)
            put('r', rows, cols, _dot(m[0], wrkv_ref[0, :, cols]).astype(BF16))
            t = -(vec[0:1, cols] + _dot(w_mid, w2_ref[:, cols]))
            sp = jnp.maximum(t, 0.0) + jnp.log(1.0 + jnp.exp(-jnp.abs(t)))
            put('lw', rows, cols, -jnp.exp(-sp - 0.5))
            yield
            put('k', rows, cols, _dot(m[2], wrkv_ref[1, :, cols]))
            put('a', rows, cols, _sigmoid(vec[1:2, cols] + _dot(a_mid, a2_ref[:, cols])))
            yield
            put('v', rows, cols, _dot(m[3], wrkv_ref[2, :, cols]).astype(BF16))
            put('gate', rows, cols, _dot(g_mid, g2_ref[:, cols]).astype(BF16))
            yield
        if nxt is not None:
            last, m = nxt
    carry_ref[...] = last
    return last


def _rwkv_proj_kernel(x_ref, g_ref, mu_ref, wrkv_ref, vec_ref, w1_ref, w2_ref, a1_ref, a2_ref,
                      g1_ref, g2_ref, sh0_ref,
                      r_ref, k_ref, v_ref, lw_ref, a_ref, gate_ref, sh_ref, carry_ref, *, sub):
    @pl.when(pl.program_id(1) == 0)
    def _():
        carry_ref[...] = sh0_ref[0]

    outs = dict(r=r_ref, k=k_ref, v=v_ref, lw=lw_ref, a=a_ref, gate=gate_ref)

    def put(name, rows, cols, val):
        outs[name][rows, cols] = val

    sh_ref[0] = _run(_proj_stages(x_ref, g_ref, mu_ref, wrkv_ref, vec_ref, w1_ref, w2_ref, a1_ref, a2_ref,
                                  g1_ref, g2_ref, carry_ref, put, sub=sub))


def _rwkv_proj(x, norm_g, i, W, j, sh0, *, nseq, seq, tm, sub):
    M, D = x.shape
    npt = seq // tm
    row = lambda b, t: (b * npt + t, 0)
    st = lambda b, t: (b, 0, 0)
    big = pl.BlockSpec((tm, D), row)
    sd = lambda dt: jax.ShapeDtypeStruct((M, D), dt)
    names = ('b_mu', 'b_w_rkv', 'b_vec', 'b_w1', 'b_w2', 'b_a1', 'b_a2', 'b_g1', 'b_g2')
    return pl.pallas_call(
        functools.partial(_rwkv_proj_kernel, sub=sub),
        grid=(nseq, npt),
        in_specs=[big, _resident(norm_g, i)] + [_resident(W[n], j) for n in names]
                 + [pl.BlockSpec((1, 1, D), st)],
        out_specs=[big, big, big, big, big, big, pl.BlockSpec((1, 1, D), st)],
        out_shape=[sd(BF16), sd(F32), sd(BF16), sd(F32), sd(F32), sd(BF16),
                   jax.ShapeDtypeStruct((nseq, 1, D), F32)],
        scratch_shapes=[pltpu.VMEM((1, D), F32)],
        compiler_params=_cparams("arbitrary", "arbitrary"),
        name="rwkv_proj",
    )(x, norm_g, *[W[n] for n in names], sh0)


def _wkv_chunk(r, k, v, lw, a, gate, prm):
    C = r[0].shape[0]
    C2 = 2 * C
    H = RWKV_HEAD
    pm = lambda f, *ls: [f(*xs) for xs in zip(*ls)]
    bf = lambda xs: [x.astype(BF16) for x in xs]
    lane = lax.broadcasted_iota(jnp.int32, (1, LANES), 1)
    low = lane < H
    hr = lax.broadcasted_iota(jnp.int32, (LANES, LANES), 0)
    hc = lax.broadcasted_iota(jnp.int32, (LANES, LANES), 1)
    head_bd = (hr < H) == (hc < H)
    ones_bd = jnp.where(head_bd, 1.0, 0.0).astype(BF16)
    fold = lambda x: x[:C] + x[C:]

    def segsum(xs):
        zs = [jnp.zeros_like(x) for x in xs]
        s0 = [jnp.sum(jnp.where(low, x, z), axis=-1, keepdims=True) for x, z in zip(xs, zs)]
        s1 = [jnp.sum(jnp.where(low, z, x), axis=-1, keepdims=True) for x, z in zip(xs, zs)]
        return [jnp.where(low, a_, b_) for a_, b_ in zip(s0, s1)]

    rowi = lax.broadcasted_iota(jnp.int32, (C, 1), 0)

    def cumsum_rows(x):
        s = 1
        while s < C:
            x = x + jnp.where(rowi >= s, pltpu.roll(x, s, axis=0), 0.0)
            s *= 2
        return x

    def stack(x):
        z = jnp.zeros_like(x)
        return jnp.concatenate([jnp.where(low, x, z), jnp.where(low, z, x)], axis=0)

    def prep():
        cum = [cumsum_rows(x) for x in lw]
        kkr = pm(lambda k_, p_: k_ * p_[0:1], k, prm)
        ss = segsum([x * x for x in kkr])
        yield
        G = [jnp.exp(c) for c in cum]
        Gm1 = pm(lambda c, l: jnp.exp(c - l), cum, lw)
        iG = [jnp.exp(-c) for c in cum]
        G_end = [jnp.exp(c[C - 1:C, :]) for c in cum]
        G_rest = [jnp.exp(c[C - 1:C, :] - c) for c in cum]
        kk = pm(lambda x, s_: x * lax.rsqrt(jnp.maximum(s_, 1e-24)), kkr, ss)
        k2 = pm(lambda k_, a_, p_: k_ * (1.0 + (a_ - 1.0) * p_[1:2]), k, a, prm)
        b = pm(lambda x, a_: x * a_, kk, a)
        At = pm(lambda x, g_: -x * g_, kk, Gm1)
        Rt = pm(lambda x, g_: x * g_, r, G)

        tt = lax.broadcasted_iota(jnp.int32, (C, C2), 0)
        ts = lax.broadcasted_iota(jnp.int32, (C, C2), 1) & (C - 1)
        strict = tt > ts
        incl = tt >= ts
        first = lax.broadcasted_iota(jnp.int32, (1, C2), 1) < C

        def bdiag(x):
            z = jnp.zeros_like(x)
            return jnp.concatenate([jnp.where(first, x, z), jnp.where(first, z, x)], axis=0).astype(BF16)

        lhs_s = pm(lambda x, y: jnp.concatenate([x, y], axis=0).astype(BF16), At, Rt)
        Bt = pm(lambda x, g_: stack(x * g_).astype(BF16), b, iG)
        Kt = pm(lambda x, g_: stack(x * g_).astype(BF16), k2, iG)
        P = pm(lambda l_, b_, k_: _dot_nt(l_, jnp.concatenate([b_, k_], axis=0)), lhs_s, Bt, Kt)
        rk = segsum(pm(lambda r_, k_, p_: r_ * k_ * p_[2:3], r, k2, prm))
        yield
        Lab = [jnp.where(strict, x[:C, :C2], 0.0) for x in P]
        Lak = [jnp.where(strict, x[:C, C2:], 0.0).astype(BF16) for x in P]
        Lrr = [jnp.concatenate([jnp.where(incl, x[C:, :C2], 0.0), jnp.where(incl, x[C:, C2:], 0.0)],
                               axis=1).astype(BF16) for x in P]
        Vst = [stack(x).astype(BF16) for x in v]
        LV = pm(_dot, Lak, Vst)

        blk = lambda s: (tt >> int(math.log2(s))) == (ts >> int(math.log2(s)))
        eye = jnp.where(tt == ts, 1.0, 0.0)
        L1 = [jnp.where(blk(8), x, 0.0) for x in Lab]
        L2 = pm(lambda x: _dot(x.astype(BF16), bdiag(x)), L1)
        yield
        L4 = pm(lambda x: _dot(x.astype(BF16), bdiag(x)), L2)
        T = pm(lambda x, y: _dot((eye + x).astype(BF16), bdiag(eye + y)), L1, L2)
        yield
        T = pm(lambda x, y: _dot(x.astype(BF16), bdiag(eye + y)), T, L4)
        yield
        s = 8
        while s < C:
            msk = blk(2 * s) & jnp.logical_not(blk(s))
            upper = [(o + s, o + 2 * s) for o in range(0, C, 2 * s)]
            Mx = [bdiag(jnp.where(msk, x, 0.0)) for x in Lab]
            Tu = [jnp.concatenate([t[a_:b_] for a_, b_ in upper], axis=0).astype(BF16) for t in T]
            TM = pm(_dot, Tu, Mx)
            yield
            X = pm(lambda tm_, t: _dot(tm_.astype(BF16), bdiag(t)), TM, T)
            yield
            T = [jnp.concatenate([piece for n, (a_, b_) in enumerate(upper)
                                  for piece in (t[a_ - s:a_], t[a_:b_] + x[n * s:(n + 1) * s])], axis=0)
                 for t, x in zip(T, X)]
            s *= 2
        rhs_s = pm(lambda b_, k_, g_: jnp.concatenate([b_ * g_, k_ * g_], axis=0).astype(BF16), b, k2, G_rest)
        return dict(Tb=bf(T), lhs_s=lhs_s, rhs_s=rhs_s, LV=LV, Lrr=Lrr, Vst=Vst, G_end=G_end, rk=rk)

    def apply(q, S):
        AZ = pm(lambda l_, s_: _dot_nt(l_, s_.astype(BF16)), q['lhs_s'], S)
        yield
        U = pm(lambda t_, az, lv: _dot(t_, stack(az[:C] + lv).astype(BF16)), q['Tb'], AZ, q['LV'])
        yield
        Y = pm(lambda az, l_, u_, v_: az[C:] + _dot(l_, jnp.concatenate([stack(u_).astype(BF16), v_], axis=0)),
               AZ, q['Lrr'], U, q['Vst'])
        upd = pm(lambda u_, v_, rhs_: _dot_tn(jnp.concatenate([u_, v_], axis=0).astype(BF16), rhs_),
                 U, v, q['rhs_s'])
        yield
        S_new = pm(lambda s_, g_, u_: s_ * g_ + jnp.where(head_bd, u_, 0.0), S, q['G_end'], upd)
        mean = [x * (1.0 / H) for x in segsum(Y)]
        yield
        d = pm(lambda y_, m_: y_ - m_, Y, mean)
        var = [x * (1.0 / H) for x in segsum([x * x for x in d])]
        yield
        out = pm(lambda d_, var_, p_, rk_, v_, g_:
                 (d_ * lax.rsqrt(var_ + GN_EPS) * p_[3:4] + p_[4:5] + rk_ * v_) * g_,
                 d, var, prm, q['rk'], v, gate)
        return out, S_new

    return prep, apply


PREP_AHEAD = 2


def _wkv_stages(load, prm_ref, sf_ref, put_z, *, tb, C):
    n_pairs = prm_ref.shape[1] // LANES
    n_chunks = tb // C
    lanes = [slice(p * LANES, (p + 1) * LANES) for p in range(n_pairs)]
    prm = [prm_ref[:, ln] for ln in lanes]

    def stages(c):
        rows = slice(c * C, (c + 1) * C)
        ld = lambda name: [load(name, rows, ln).astype(F32) for ln in lanes]
        return _wkv_chunk(*[ld(name) for name, _ in PROJ_OUT], prm)

    S = [sf_ref[0, p] for p in range(n_pairs)]
    applies, ready, in_flight = {}, {}, []
    launched = done = 0
    cur = None
    while done < n_chunks:
        while len(in_flight) < PREP_AHEAD and launched < n_chunks:
            prep, applies[launched] = stages(launched)
            in_flight.append((launched, prep()))
            launched += 1
        if cur is None and done in ready:
            cur = applies.pop(done)(ready.pop(done), S)
        if cur is not None:
            finished, res = _step(cur)
            if finished:
                z, S = res
                for p, ln in enumerate(lanes):
                    put_z(slice(done * C, (done + 1) * C), ln, z[p].astype(BF16))
                cur = None
                done += 1
        for c, gen in list(in_flight):
            finished, res = _step(gen)
            if finished:
                ready[c] = res
                in_flight.remove((c, gen))
        yield
    for p in range(n_pairs):
        sf_ref[0, p] = S[p]


def _wkv_kernel(r_ref, k_ref, v_ref, lw_ref, a_ref, g_ref, prm_ref, s0_ref, z_ref, sf_ref, *, C):
    @pl.when(pl.program_id(1) == 0)
    def _():
        sf_ref[...] = s0_ref[...]

    refs = dict(r=r_ref, k=k_ref, v=v_ref, lw=lw_ref, a=a_ref, gate=g_ref)

    def put_z(rows, ln, val):
        z_ref[rows, ln] = val

    _run(_wkv_stages(lambda name, rows, ln: refs[name][rows, ln], prm_ref, sf_ref, put_z,
                     tb=r_ref.shape[0], C=C))


def _wkv(r, k, v, lw, a, gate, prm, j, s0, *, nseq, seq, tb, C):
    M, D = r.shape
    npt = seq // tb
    row = lambda b, t: (b * npt + t, 0)
    big = pl.BlockSpec((tb, D), row)
    st = pl.BlockSpec((1,) + s0.shape[1:], lambda b, t: (b, 0, 0, 0))
    return pl.pallas_call(
        functools.partial(_wkv_kernel, C=C),
        grid=(nseq, npt),
        in_specs=[big, big, big, big, big, big, _resident(prm, j), st],
        out_specs=[big, st],
        out_shape=[jax.ShapeDtypeStruct((M, D), BF16), jax.ShapeDtypeStruct(s0.shape, F32)],
        compiler_params=_cparams("arbitrary", "arbitrary"),
        name="rwkv_wkv",
    )(r, k, v, lw, a, gate, prm, s0)


def _pair_state(s):
    B, Hh, N, _ = s.shape
    s = s.reshape(B, Hh // 2, 2, N, N)
    z = jnp.zeros((B, Hh // 2, N, N), s.dtype)
    return jnp.concatenate([jnp.concatenate([s[:, :, 0], z], axis=-1),
                            jnp.concatenate([z, s[:, :, 1]], axis=-1)], axis=-2)


def _unpair_state(z):
    N = z.shape[-1] // 2
    return jnp.stack([z[:, :, :N, :N], z[:, :, N:, N:]], axis=2).reshape(z.shape[0], -1, N, N)


TILE_ROWS = dict(post=512, seq=512, rwkv_proj=512, rwkv_sub=512, wkv=512)


def _tile(n, pref):
    t = min(n, pref)
    assert n % t == 0
    return t


def _trunk(x3, prompt, a_k, a_v, b_wkv, b_shift, c_conv, W):
    B, T, D = x3.shape
    M = B * T
    x = x3.reshape(M, D)
    norm_g = W['norm_g']
    n_heads = D // HEAD_DIM
    n_kv = W['n_kv']
    kvw = n_kv * HEAD_DIM
    nk, nv, nwkv, nsh, ncv = [], [], [], [], []
    seq_tile = _tile(T, TILE_ROWS['seq'])
    for i in range(norm_g.shape[0]):
        kind, j = i % N_MIXERS, i // N_MIXERS
        if kind == 0:
            q, kv, tail = _qkv(x, norm_g, i, W['a_w_qkv'], j, seq=T, tm=seq_tile)
            tail_k = tail[:, :, :kvw].reshape(B, -1, n_kv, HEAD_DIM)
            tail_v = tail[:, :, kvw:].reshape(B, -1, n_kv, HEAD_DIM)
            if prompt:
                nk.append(tail_k)
                nv.append(tail_v)
                x = _attn_post(q, kv, x, W['rel_bias_table'], W['a_sinks'][j], W['a_w_o'], j, norm_g,
                               W['mlp_w1'], W['mlp_w2'], i, seq=T, tq=seq_tile, qc=CHUNK)
                continue
            ck, cv = a_k[j], a_v[j]
            prev = jnp.concatenate([ck.reshape(B * WINDOW, kvw), cv.reshape(B * WINDOW, kvw)], axis=-1).astype(BF16)
            z = _attn(q, kv, prev, lambda b, t: (b, 0), W['rel_bias_table'], W['a_sinks'][j], nseq=B, seq=T,
                      tq=seq_tile, qc=T, masked=False)
            nk.append(jnp.concatenate([ck, tail_k], axis=1)[:, -WINDOW:])
            nv.append(jnp.concatenate([cv, tail_v], axis=1)[:, -WINDOW:])
            w_out = W['a_w_o']
        elif kind == 1:
            if prompt:
                sh0 = jnp.zeros((B, 1, D), F32)
                s0 = jnp.zeros((B, n_heads // 2, LANES, LANES), F32)
            else:
                sh0 = b_shift[j].reshape(B, 1, D)
                s0 = _pair_state(b_wkv[j])
            tm = _tile(T, TILE_ROWS['rwkv_proj'])
            r, k, v, lw, a, gate, sh_new = _rwkv_proj(x, norm_g, i, W, j, sh0, nseq=B, seq=T, tm=tm,
                                                      sub=_tile(tm, TILE_ROWS['rwkv_sub']))
            nsh.append(sh_new.reshape(B, D))
            z, s_fin = _wkv(r, k, v, lw, a, gate, W['b_prm'], j, s0, nseq=B, seq=T,
                            tb=_tile(T, TILE_ROWS['wkv']), C=min(T, CHUNK))
            nwkv.append(_unpair_state(s_fin))
            w_out = W['b_w_o']
        else:
            u0 = jnp.zeros((B, CONV_WIDTH - 1, D), F32) if prompt else c_conv[j]
            z, u_new = _conv(x, norm_g, i, W['c_w_in'], W['c_conv_w'], j, u0, nseq=B, seq=T, tm=seq_tile)
            ncv.append(u_new)
            w_out = W['c_w_out']
        x = _post(x, z, w_out, j, norm_g, W['mlp_w1'], W['mlp_w2'], i, tm=_tile(M, TILE_ROWS['post']))
    return (x.reshape(B, T, D), jnp.stack(nk), jnp.stack(nv), jnp.stack(nwkv), jnp.stack(nsh),
            jnp.stack(ncv))


def kernel(x_prompt, x_sample, cache_a_k, cache_a_v, state_b_wkv, state_b_shift, state_c_conv, rel_bias_table, norm_g, a_w_qkv, a_w_o, a_sinks, b_mu, b_w_rkv, b_w_o, b_w0, b_w1, b_w2, b_a0, b_a1, b_a2, b_g1, b_g2, b_k_k, b_k_a, b_r_k, b_ln_w, b_ln_b, c_w_in, c_conv_w, c_w_out, mlp_w1, mlp_w2):
    D = x_prompt.shape[-1]
    n_kv = cache_a_k.shape[3]
    nq = D
    bf = lambda t: t.astype(BF16)
    order = jnp.array(_head_order(nq // HEAD_DIM, n_kv))
    n_a = a_w_qkv.shape[0]
    wq = (a_w_qkv[..., :nq] * (HEAD_DIM ** -0.5 * LOG2E)).reshape(n_a, D, -1, HEAD_DIM)[:, :, order]
    w_qkv = bf(jnp.concatenate([wq.reshape(n_a, D, nq), a_w_qkv[..., nq:]], axis=-1))
    a_w_o = a_w_o.reshape(n_a, -1, HEAD_DIM, D)[:, order].reshape(a_w_o.shape)
    n_b = b_mu.shape[0]
    zeros = jnp.zeros_like(b_w0)
    W = dict(
        n_kv=n_kv, rel_bias_table=rel_bias_table, norm_g=norm_g, a_w_qkv=w_qkv, a_w_o=bf(a_w_o), a_sinks=a_sinks,
        b_mu=b_mu, b_w_rkv=bf(b_w_rkv), b_w_o=bf(b_w_o),
        b_vec=jnp.stack([b_w0, b_a0], axis=1),
        b_w1=bf(b_w1), b_w2=bf(b_w2), b_a1=bf(b_a1), b_a2=bf(b_a2), b_g1=bf(b_g1), b_g2=bf(b_g2),
        b_prm=jnp.stack([b_k_k, b_k_a, b_r_k.reshape(n_b, D), b_ln_w, b_ln_b, zeros, zeros, zeros], axis=1),
        c_w_in=bf(c_w_in), c_conv_w=c_conv_w, c_w_out=bf(c_w_out), mlp_w1=bf(mlp_w1), mlp_w2=bf(mlp_w2))
    y_p, ak_p, av_p, wkv_p, sh_p, cv_p = _trunk(x_prompt, True, None, None, None, None, None, W)
    y_s, ak_s, av_s, wkv_s, sh_s, cv_s = _trunk(x_sample, False, cache_a_k, cache_a_v, state_b_wkv,
                                                state_b_shift, state_c_conv, W)
    return (y_p, y_s, ak_p, av_p, ak_s, av_s, wkv_p, wkv_s, sh_p, sh_s, cv_p, cv_s)
```

```python
import functools
import math

import jax
import jax.numpy as jnp
from jax import lax
from jax.experimental import pallas as pl
from jax.experimental.pallas import tpu as pltpu

F32 = jnp.float32
BF16 = jnp.bfloat16

HEAD_DIM = 64
CHUNK = 64
WINDOW = 128
NUM_BUCKETS = 32
MAX_DISTANCE = 128
RWKV_HEAD = 64
CONV_WIDTH = 3
N_MIXERS = 3
RMS_EPS = 1e-6
GN_EPS = RWKV_HEAD * 1e-5
LOG2E = math.log2(math.e)

LANES = 128
VMEM_LIMIT = 56 * 1024 * 1024


def _cparams(*sem):
    return pltpu.CompilerParams(dimension_semantics=sem, vmem_limit_bytes=VMEM_LIMIT)


def _resident(arr, layer=None):
    if layer is None:
        nd = arr.ndim
        return pl.BlockSpec(arr.shape, lambda *_: (0,) * nd, pipeline_mode=pl.Buffered(1))
    nd = arr.ndim - 1
    return pl.BlockSpec((None,) + arr.shape[1:], lambda *_: (layer,) + (0,) * nd, pipeline_mode=pl.Buffered(1))


def _rms(x, g):
    return x * lax.rsqrt(jnp.mean(x * x, axis=-1, keepdims=True) + RMS_EPS) * g


def _dot(a, b):
    return jnp.dot(a, b, preferred_element_type=F32)


def _dot_nt(a, b):
    return lax.dot_general(a, b, (((1,), (1,)), ((), ())), preferred_element_type=F32)


def _dot_tn(a, b):
    return lax.dot_general(a, b, (((0,), (0,)), ((), ())), preferred_element_type=F32)


def _sigmoid(x):
    return 1.0 / (1.0 + jnp.exp(-x))


def _step(gen):
    try:
        next(gen)
        return False, None
    except StopIteration as stop:
        return True, stop.value


def _run(*gens, strides=None):
    strides = strides or [1] * len(gens)
    results = [None] * len(gens)
    live = list(range(len(gens)))
    while live:
        for n in list(live):
            for _ in range(strides[n]):
                finished, value = _step(gens[n])
                if finished:
                    results[n] = value
                    live.remove(n)
                    break
    return results[0] if len(gens) == 1 else results


def _post_stages(x_ref, load_z, wo_ref, g_ref, w1_ref, w2_ref, o_ref, acc_ref, *, ff_chunk):
    g = g_ref[...]
    m = _dot(load_z(), wo_ref[...])
    x1 = x_ref[...] + _rms(m, g[1:2])
    h2 = _rms(x1, g[2:3]).astype(BF16)
    yield
    d_ff = w1_ref.shape[1]
    for c in range(d_ff // ff_chunk):
        sl = slice(c * ff_chunk, (c + 1) * ff_chunk)
        a = jnp.maximum(_dot(h2, w1_ref[:, sl]), 0.0)
        a = (a * a).astype(BF16)
        yield
        part = _dot(a, w2_ref[sl, :])
        if c == 0:
            acc_ref[...] = part
        else:
            acc_ref[...] += part
        yield
    o_ref[...] = x1 + _rms(acc_ref[...], g[3:4])


def _post_kernel(x_ref, z_ref, wo_ref, g_ref, w1_ref, w2_ref, o_ref, acc_ref, *, ff_chunk):
    _run(_post_stages(x_ref, lambda: z_ref[...], wo_ref, g_ref, w1_ref, w2_ref, o_ref, acc_ref,
                      ff_chunk=ff_chunk))


def _post(x, z, w_out, j, norm_g, w1, w2, i, *, tm):
    M, D = x.shape
    d_ff = w1.shape[2]
    row = lambda t: (t, 0)
    return pl.pallas_call(
        functools.partial(_post_kernel, ff_chunk=min(d_ff, 1024)),
        grid=(M // tm,),
        in_specs=[pl.BlockSpec((tm, D), row), pl.BlockSpec((tm, D), row),
                  _resident(w_out, j), _resident(norm_g, i), _resident(w1, i), _resident(w2, i)],
        out_specs=pl.BlockSpec((tm, D), row),
        out_shape=jax.ShapeDtypeStruct((M, D), F32),
        scratch_shapes=[pltpu.VMEM((tm, D), F32)],
        compiler_params=_cparams("parallel"),
        name="post_mlp",
    )(x, z, w_out, norm_g, w1, w2)


def _qkv_kernel(x_ref, g_ref, w_ref, q_ref, kv_ref, tail_ref, *, tiles_per_seq, tail):
    h = _rms(x_ref[...], g_ref[0:1, :]).astype(BF16)
    p = _dot(h, w_ref[...])
    D = q_ref.shape[1]
    q_ref[...] = p[:, :D].astype(BF16)
    kv_ref[...] = p[:, D:].astype(BF16)
    tm = p.shape[0]

    @pl.when(pl.program_id(0) % tiles_per_seq == tiles_per_seq - 1)
    def _():
        tail_ref[0] = p[tm - tail:, D:]


def _qkv(x, norm_g, i, w, j, *, seq, tm):
    M, D = x.shape
    KV = w.shape[2] - D
    tail = min(WINDOW, seq)
    assert seq % tm == 0 and tail <= tm
    tps = seq // tm
    row = lambda i: (i, 0)
    return pl.pallas_call(
        functools.partial(_qkv_kernel, tiles_per_seq=tps, tail=tail),
        grid=(M // tm,),
        in_specs=[pl.BlockSpec((tm, D), row), _resident(norm_g, i), _resident(w, j)],
        out_specs=[pl.BlockSpec((tm, D), row), pl.BlockSpec((tm, KV), row),
                   pl.BlockSpec((1, tail, KV), lambda t: (t // tps, 0, 0))],
        out_shape=[jax.ShapeDtypeStruct((M, D), BF16), jax.ShapeDtypeStruct((M, KV), BF16),
                   jax.ShapeDtypeStruct((M // seq, tail, KV), F32)],
        compiler_params=_cparams("arbitrary"),
        name="attn_qkv",
    )(x, norm_g, w)


def _head_order(n_heads, n_kv):
    group = n_heads // n_kv
    return tuple(group * (2 * g + half) + i for g in range(n_kv // 2) for i in range(group) for half in (0, 1))


def _attn_kernel(sink_ref, table_ref, bucket_ref, q_ref, prev_ref, cur_ref, o_ref, bias_ref, *,
                 qc, masked):
    @pl.when((pl.program_id(0) == 0) & (pl.program_id(1) == 0))
    def _():
        _attn_bias_init(table_ref, bucket_ref, bias_ref, cur_ref.shape[1] // 2 // HEAD_DIM, qc=qc, masked=masked)

    def put(rows, lanes, val):
        o_ref[rows, lanes] = val

    _run(_attn_stages(sink_ref, q_ref, prev_ref, cur_ref, bias_ref, put, qc=qc,
                      first_chunk=pl.program_id(1) * (q_ref.shape[0] // qc)))


def _attn_bias_init(table_ref, bucket_ref, bias_ref, n_kv, *, qc, masked):
    n_var, n_pairs, _, band = bias_ref.shape
    keypos = lax.broadcasted_iota(jnp.int32, (1, band), 1)
    bucket = bucket_ref[...]
    for slot, h in enumerate(_head_order(2 * n_pairs, n_kv)):
        pick = lambda b, acc: jnp.where(bucket == b, table_ref[b, h], acc)
        bias_h = lax.fori_loop(0, table_ref.shape[0], pick, jnp.zeros((qc, band), F32)) * LOG2E
        for v in range(n_var):
            hidden = keypos < (WINDOW - v * qc if masked else 0)
            bias_ref[v, slot // 2, (slot % 2) * qc:(slot % 2 + 1) * qc, :] = jnp.where(hidden, -jnp.inf, bias_h)


def _attn_stages(sink_ref, q_ref, prev_ref, cur_ref, bias_ref, put, *, qc, first_chunk):
    tq, D = q_ref.shape
    kvw = cur_ref.shape[1] // 2
    band = WINDOW + qc
    n_var = bias_ref.shape[0]
    n_pairs = D // LANES
    pairs_per_kv = n_pairs // (kvw // LANES)
    order = _head_order(2 * n_pairs, kvw // HEAD_DIM)
    lane = lax.broadcasted_iota(jnp.int32, (1, LANES), 1)
    low = lane < HEAD_DIM
    row2 = lax.broadcasted_iota(jnp.int32, (2 * qc, 1), 0)
    pairs = range(n_pairs)
    sink = [jnp.where(row2 < qc, sink_ref[order[2 * p]], sink_ref[order[2 * p + 1]]) * LOG2E for p in pairs]

    for c in range(tq // qc):
        qrows = q_ref[c * qc:(c + 1) * qc, :]
        r0 = c * qc
        if r0 >= WINDOW:
            kvb = cur_ref[r0 - WINDOW:r0 + qc, :]
        else:
            kvb = jnp.concatenate([prev_ref[r0:, :], cur_ref[0:r0 + qc, :]], axis=0)
        var = jnp.minimum(first_chunk + c, n_var - 1)

        def scores(p):
            qp = qrows[:, p * LANES:(p + 1) * LANES]
            zero = jnp.zeros_like(qp)
            lhs = jnp.concatenate([jnp.where(low, qp, zero), jnp.where(low, zero, qp)], axis=0)
            hk = p // pairs_per_kv
            return _dot_nt(lhs, kvb[:, hk * LANES:(hk + 1) * LANES]) + bias_ref[var, p]

        s = [scores(p) for p in pairs]
        m = [jnp.maximum(jnp.max(s[p], axis=-1, keepdims=True), sink[p]) for p in pairs]
        e = [jnp.exp2(s[p] - m[p]) for p in pairs]
        den = [jnp.sum(e[p], axis=-1, keepdims=True) + jnp.exp2(sink[p] - m[p]) for p in pairs]
        yield
        o2 = [_dot(e[p].astype(BF16),
                   kvb[:, kvw + (p // pairs_per_kv) * LANES: kvw + (p // pairs_per_kv + 1) * LANES]) for p in pairs]
        for p in pairs:
            o = o2[p] / den[p]
            put(slice(c * qc, (c + 1) * qc), slice(p * LANES, (p + 1) * LANES),
                jnp.where(low, o[:qc], o[qc:]).astype(BF16))
        yield


def _bucket_map(qc):
    qi = jnp.arange(qc)[:, None]
    kj = jnp.arange(WINDOW + qc)[None, :]
    return _rel_bucket(kj - WINDOW - qi)


def _attn(q, kv, prev, prev_map, table, sinks, *, nseq, seq, tq, qc, masked):
    M, D = q.shape
    KV2 = kv.shape[1]
    npt = seq // tq
    n_var = WINDOW // qc + 1 if masked else 1
    bucket = _bucket_map(qc)
    row = lambda b, t: (b * npt + t, 0)
    smem = pl.BlockSpec(memory_space=pltpu.SMEM)
    return pl.pallas_call(
        functools.partial(_attn_kernel, qc=qc, masked=masked),
        grid=(nseq, npt),
        in_specs=[smem, smem, _resident(bucket),
                  pl.BlockSpec((tq, D), row),
                  pl.BlockSpec((WINDOW, KV2), prev_map),
                  pl.BlockSpec((tq, KV2), row)],
        out_specs=pl.BlockSpec((tq, D), row),
        out_shape=jax.ShapeDtypeStruct((M, D), BF16),
        scratch_shapes=[pltpu.VMEM((n_var, D // LANES, 2 * qc, WINDOW + qc), F32)],
        compiler_params=_cparams("arbitrary", "arbitrary"),
        name="attn_core",
    )(sinks, table, bucket, q, prev, kv)


def _attn_post_kernel(sink_ref, table_ref, bucket_ref, q_ref, prev_ref, cur_ref, x_ref, wo_ref, g_ref, w1_ref,
                      w2_ref, out_ref, bias_ref, o_scr, acc_ref, *, qc, tiles_per_seq, n_tiles, ff_chunk):
    t = pl.program_id(0)

    @pl.when(t == 0)
    def _():
        _attn_bias_init(table_ref, bucket_ref, bias_ref, cur_ref.shape[1] // 2 // HEAD_DIM, qc=qc, masked=True)
        o_scr[...] = jnp.zeros_like(o_scr)

    tile = jnp.minimum(t, n_tiles - 1)
    slot_w = t % 2
    slot_r = (t + 1) % 2

    def put(rows, lanes, val):
        o_scr[slot_w, rows, lanes] = val

    attn = _attn_stages(sink_ref, q_ref, prev_ref, cur_ref, bias_ref, put, qc=qc,
                        first_chunk=(tile % tiles_per_seq) * (q_ref.shape[0] // qc))
    post = _post_stages(x_ref, lambda: o_scr[slot_r], wo_ref, g_ref, w1_ref, w2_ref, out_ref, acc_ref,
                        ff_chunk=ff_chunk)
    _run(post, attn)


def _attn_post(q, kv, x, table, sinks, w_out, j, norm_g, w1, w2, i, *, seq, tq, qc):
    M, D = q.shape
    KV2 = kv.shape[1]
    d_ff = w1.shape[2]
    n_tiles = M // tq
    n_var = WINDOW // qc + 1
    bucket = _bucket_map(qc)
    cur = lambda t: (jnp.minimum(t, n_tiles - 1), 0)
    prev = lambda t: (jnp.maximum(jnp.minimum(t, n_tiles - 1) * (tq // WINDOW) - 1, 0), 0)
    lag = lambda t: (jnp.maximum(t - 1, 0), 0)
    smem = pl.BlockSpec(memory_space=pltpu.SMEM)
    return pl.pallas_call(
        functools.partial(_attn_post_kernel, qc=qc, tiles_per_seq=seq // tq, n_tiles=n_tiles,
                          ff_chunk=min(d_ff, 512)),
        grid=(n_tiles + 1,),
        in_specs=[smem, smem, _resident(bucket),
                  pl.BlockSpec((tq, D), cur), pl.BlockSpec((WINDOW, KV2), prev), pl.BlockSpec((tq, KV2), cur),
                  pl.BlockSpec((tq, D), lag),
                  _resident(w_out, j), _resident(norm_g, i), _resident(w1, i), _resident(w2, i)],
        out_specs=pl.BlockSpec((tq, D), lag),
        out_shape=jax.ShapeDtypeStruct((M, D), F32),
        scratch_shapes=[pltpu.VMEM((n_var, D // LANES, 2 * qc, WINDOW + qc), F32),
                        pltpu.VMEM((2, tq, D), BF16),
                        pltpu.VMEM((tq, D), F32)],
        compiler_params=_cparams("arbitrary"),
        name="attn_post_mlp",
    )(sinks, table, bucket, q, kv, kv, x, w_out, norm_g, w1, w2)


def _rel_bucket(rp):
    nb = NUM_BUCKETS // 2
    ret = (rp > 0).astype(jnp.int32) * nb
    n = jnp.abs(rp)
    max_exact = nb // 2
    nf = jnp.maximum(n, 1).astype(F32)
    large = max_exact + (jnp.log(nf / max_exact) / math.log(MAX_DISTANCE / max_exact)
                         * (nb - max_exact)).astype(jnp.int32)
    large = jnp.minimum(large, nb - 1)
    return ret + jnp.where(n < max_exact, n, large)


CONV_COLS = 256
PROJ_COLS = 256


def _conv_kernel(x_ref, g_ref, w_ref, cw_ref, u0_ref, z_ref, un_ref, carry_ref):
    @pl.when(pl.program_id(1) == 0)
    def _():
        carry_ref[...] = u0_ref[0]

    tm, D = x_ref.shape
    h = _rms(x_ref[...], g_ref[0:1, :]).astype(BF16)
    row = lax.broadcasted_iota(jnp.int32, (tm, 1), 0)
    for c in range(0, D, CONV_COLS):
        cols = slice(c, c + CONV_COLS)
        bg = _dot(h, w_ref[:, c:c + CONV_COLS])
        u = _dot(h, w_ref[:, D + c:D + c + CONV_COLS]) * _dot(h, w_ref[:, 2 * D + c:2 * D + c + CONV_COLS])
        c0, c1 = carry_ref[0:1, cols], carry_ref[1:2, cols]
        um1 = jnp.where(row == 0, c1, pltpu.roll(u, 1, axis=0))
        um2 = jnp.where(row == 0, c0, jnp.where(row == 1, c1, pltpu.roll(u, 2, axis=0)))
        cw = cw_ref[:, cols]
        y = um2 * cw[0:1] + um1 * cw[1:2] + u * cw[2:3]
        z_ref[:, cols] = (bg * y).astype(BF16)
        carry_ref[:, cols] = u[tm - 2:, :]
        un_ref[0, :, cols] = u[tm - 2:, :]


def _conv(x, norm_g, i, w_in, cw, j, u0, *, nseq, seq, tm):
    M, D = x.shape
    npt = seq // tm
    row = lambda b, t: (b * npt + t, 0)
    st = lambda b, t: (b, 0, 0)
    return pl.pallas_call(
        _conv_kernel,
        grid=(nseq, npt),
        in_specs=[pl.BlockSpec((tm, D), row), _resident(norm_g, i), _resident(w_in, j),
                  _resident(cw, j), pl.BlockSpec((1, CONV_WIDTH - 1, D), st)],
        out_specs=[pl.BlockSpec((tm, D), row), pl.BlockSpec((1, CONV_WIDTH - 1, D), st)],
        out_shape=[jax.ShapeDtypeStruct((M, D), BF16),
                   jax.ShapeDtypeStruct((nseq, CONV_WIDTH - 1, D), F32)],
        scratch_shapes=[pltpu.VMEM((CONV_WIDTH - 1, D), F32)],
        compiler_params=_cparams("arbitrary", "arbitrary"),
        name="conv_mix",
    )(x, norm_g, w_in, cw, u0)


PROJ_OUT = (('r', BF16), ('k', F32), ('v', BF16), ('lw', F32), ('a', F32), ('gate', BF16))


def _proj_stages(x_ref, g_ref, mu_ref, wrkv_ref, vec_ref, w1_ref, w2_ref, a1_ref, a2_ref, g1_ref, g2_ref,
                 carry_ref, put, *, sub):
    tm = x_ref.shape[0]
    mu = mu_ref[...]
    vec = vec_ref[...]
    row = lax.broadcasted_iota(jnp.int32, (sub, 1), 0)

    def mixes(s, prev_last):
        h = _rms(x_ref[s * sub:(s + 1) * sub, :], g_ref[0:1, :])
        xx = jnp.where(row == 0, prev_last, pltpu.roll(h, 1, axis=0)) - h
        return h[sub - 1:, :], [(h + xx * mu[i:i + 1]).astype(BF16) for i in range(6)]

    last, m = mixes(0, carry_ref[...])
    yield
    for s in range(tm // sub):
        rows = slice(s * sub, (s + 1) * sub)
        nxt = mixes(s + 1, last) if (s + 1) * sub < tm else None
        w_mid = jnp.tanh(_dot(m[1], w1_ref[...])).astype(BF16)
        a_mid = _dot(m[4], a1_ref[...]).astype(BF16)
        g_mid = _sigmoid(_dot(m[5], g1_ref[...])).astype(BF16)
        yield
        for c0 in range(0, wrkv_ref.shape[2], PROJ_COLS):
            cols = slice(c0, c0 + PROJ_COLS)
            put('r', rows, cols, _dot(m[0], wrkv_ref[0, :, cols]).astype(BF16))
            t = -(vec[0:1, cols] + _dot(w_mid, w2_ref[:, cols]))
            sp = jnp.maximum(t, 0.0) + jnp.log(1.0 + jnp.exp(-jnp.abs(t)))
            put('lw', rows, cols, -jnp.exp(-sp - 0.5))
            yield
            put('k', rows, cols, _dot(m[2], wrkv_ref[1, :, cols]))
            put('a', rows, cols, _sigmoid(vec[1:2, cols] + _dot(a_mid, a2_ref[:, cols])))
            yield
            put('v', rows, cols, _dot(m[3], wrkv_ref[2, :, cols]).astype(BF16))
            put('gate', rows, cols, _dot(g_mid, g2_ref[:, cols]).astype(BF16))
            yield
        if nxt is not None:
            last, m = nxt
    carry_ref[...] = last
    return last


def _rwkv_proj_kernel(x_ref, g_ref, mu_ref, wrkv_ref, vec_ref, w1_ref, w2_ref, a1_ref, a2_ref,
                      g1_ref, g2_ref, sh0_ref,
                      r_ref, k_ref, v_ref, lw_ref, a_ref, gate_ref, sh_ref, carry_ref, *, sub):
    @pl.when(pl.program_id(1) == 0)
    def _():
        carry_ref[...] = sh0_ref[0]

    outs = dict(r=r_ref, k=k_ref, v=v_ref, lw=lw_ref, a=a_ref, gate=gate_ref)

    def put(name, rows, cols, val):
        outs[name][rows, cols] = val

    sh_ref[0] = _run(_proj_stages(x_ref, g_ref, mu_ref, wrkv_ref, vec_ref, w1_ref, w2_ref, a1_ref, a2_ref,
                                  g1_ref, g2_ref, carry_ref, put, sub=sub))


def _rwkv_proj(x, norm_g, i, W, j, sh0, *, nseq, seq, tm, sub):
    M, D = x.shape
    npt = seq // tm
    row = lambda b, t: (b * npt + t, 0)
    st = lambda b, t: (b, 0, 0)
    big = pl.BlockSpec((tm, D), row)
    sd = lambda dt: jax.ShapeDtypeStruct((M, D), dt)
    names = ('b_mu', 'b_w_rkv', 'b_vec', 'b_w1', 'b_w2', 'b_a1', 'b_a2', 'b_g1', 'b_g2')
    return pl.pallas_call(
        functools.partial(_rwkv_proj_kernel, sub=sub),
        grid=(nseq, npt),
        in_specs=[big, _resident(norm_g, i)] + [_resident(W[n], j) for n in names]
                 + [pl.BlockSpec((1, 1, D), st)],
        out_specs=[big, big, big, big, big, big, pl.BlockSpec((1, 1, D), st)],
        out_shape=[sd(BF16), sd(F32), sd(BF16), sd(F32), sd(F32), sd(BF16),
                   jax.ShapeDtypeStruct((nseq, 1, D), F32)],
        scratch_shapes=[pltpu.VMEM((1, D), F32)],
        compiler_params=_cparams("arbitrary", "arbitrary"),
        name="rwkv_proj",
    )(x, norm_g, *[W[n] for n in names], sh0)


def _wkv_chunk(r, k, v, lw, a, gate, prm):
    C = r[0].shape[0]
    C2 = 2 * C
    H = RWKV_HEAD
    pm = lambda f, *ls: [f(*xs) for xs in zip(*ls)]
    bf = lambda xs: [x.astype(BF16) for x in xs]
    lane = lax.broadcasted_iota(jnp.int32, (1, LANES), 1)
    low = lane < H
    hr = lax.broadcasted_iota(jnp.int32, (LANES, LANES), 0)
    hc = lax.broadcasted_iota(jnp.int32, (LANES, LANES), 1)
    head_bd = (hr < H) == (hc < H)
    ones_bd = jnp.where(head_bd, 1.0, 0.0).astype(BF16)
    fold = lambda x: x[:C] + x[C:]

    def segsum(xs):
        zs = [jnp.zeros_like(x) for x in xs]
        s0 = [jnp.sum(jnp.where(low, x, z), axis=-1, keepdims=True) for x, z in zip(xs, zs)]
        s1 = [jnp.sum(jnp.where(low, z, x), axis=-1, keepdims=True) for x, z in zip(xs, zs)]
        return [jnp.where(low, a_, b_) for a_, b_ in zip(s0, s1)]

    rowi = lax.broadcasted_iota(jnp.int32, (C, 1), 0)

    def cumsum_rows(x):
        s = 1
        while s < C:
            x = x + jnp.where(rowi >= s, pltpu.roll(x, s, axis=0), 0.0)
            s *= 2
        return x

    def stack(x):
        z = jnp.zeros_like(x)
        return jnp.concatenate([jnp.where(low, x, z), jnp.where(low, z, x)], axis=0)

    def prep():
        cum = [cumsum_rows(x) for x in lw]
        kkr = pm(lambda k_, p_: k_ * p_[0:1], k, prm)
        ss = segsum([x * x for x in kkr])
        yield
        G = [jnp.exp(c) for c in cum]
        Gm1 = pm(lambda c, l: jnp.exp(c - l), cum, lw)
        iG = [jnp.exp(-c) for c in cum]
        G_end = [jnp.exp(c[C - 1:C, :]) for c in cum]
        G_rest = [jnp.exp(c[C - 1:C, :] - c) for c in cum]
        kk = pm(lambda x, s_: x * lax.rsqrt(jnp.maximum(s_, 1e-24)), kkr, ss)
        k2 = pm(lambda k_, a_, p_: k_ * (1.0 + (a_ - 1.0) * p_[1:2]), k, a, prm)
        b = pm(lambda x, a_: x * a_, kk, a)
        At = pm(lambda x, g_: -x * g_, kk, Gm1)
        Rt = pm(lambda x, g_: x * g_, r, G)

        tt = lax.broadcasted_iota(jnp.int32, (C, C2), 0)
        ts = lax.broadcasted_iota(jnp.int32, (C, C2), 1) & (C - 1)
        strict = tt > ts
        incl = tt >= ts
        first = lax.broadcasted_iota(jnp.int32, (1, C2), 1) < C

        def bdiag(x):
            z = jnp.zeros_like(x)
            return jnp.concatenate([jnp.where(first, x, z), jnp.where(first, z, x)], axis=0).astype(BF16)

        lhs_s = pm(lambda x, y: jnp.concatenate([x, y], axis=0).astype(BF16), At, Rt)
        Bt = pm(lambda x, g_: stack(x * g_).astype(BF16), b, iG)
        Kt = pm(lambda x, g_: stack(x * g_).astype(BF16), k2, iG)
        P = pm(lambda l_, b_, k_: _dot_nt(l_, jnp.concatenate([b_, k_], axis=0)), lhs_s, Bt, Kt)
        rk = segsum(pm(lambda r_, k_, p_: r_ * k_ * p_[2:3], r, k2, prm))
        yield
        Lab = [jnp.where(strict, x[:C, :C2], 0.0) for x in P]
        Lak = [jnp.where(strict, x[:C, C2:], 0.0).astype(BF16) for x in P]
        Lrr = [jnp.concatenate([jnp.where(incl, x[C:, :C2], 0.0), jnp.where(incl, x[C:, C2:], 0.0)],
                               axis=1).astype(BF16) for x in P]
        Vst = [stack(x).astype(BF16) for x in v]
        LV = pm(_dot, Lak, Vst)

        blk = lambda s: (tt >> int(math.log2(s))) == (ts >> int(math.log2(s)))
        eye = jnp.where(tt == ts, 1.0, 0.0)
        L1 = [jnp.where(blk(8), x, 0.0) for x in Lab]
        L2 = pm(lambda x: _dot(x.astype(BF16), bdiag(x)), L1)
        yield
        L4 = pm(lambda x: _dot(x.astype(BF16), bdiag(x)), L2)
        T = pm(lambda x, y: _dot((eye + x).astype(BF16), bdiag(eye + y)), L1, L2)
        yield
        T = pm(lambda x, y: _dot(x.astype(BF16), bdiag(eye + y)), T, L4)
        yield
        s = 8
        while s < C:
            msk = blk(2 * s) & jnp.logical_not(blk(s))
            upper = [(o + s, o + 2 * s) for o in range(0, C, 2 * s)]
            Mx = [bdiag(jnp.where(msk, x, 0.0)) for x in Lab]
            Tu = [jnp.concatenate([t[a_:b_] for a_, b_ in upper], axis=0).astype(BF16) for t in T]
            TM = pm(_dot, Tu, Mx)
            yield
            X = pm(lambda tm_, t: _dot(tm_.astype(BF16), bdiag(t)), TM, T)
            yield
            T = [jnp.concatenate([piece for n, (a_, b_) in enumerate(upper)
                                  for piece in (t[a_ - s:a_], t[a_:b_] + x[n * s:(n + 1) * s])], axis=0)
                 for t, x in zip(T, X)]
            s *= 2
        rhs_s = pm(lambda b_, k_, g_: jnp.concatenate([b_ * g_, k_ * g_], axis=0).astype(BF16), b, k2, G_rest)
        return dict(Tb=bf(T), lhs_s=lhs_s, rhs_s=rhs_s, LV=LV, Lrr=Lrr, Vst=Vst, G_end=G_end, rk=rk)

    def apply(q, S):
        AZ = pm(lambda l_, s_: _dot_nt(l_, s_.astype(BF16)), q['lhs_s'], S)
        yield
        U = pm(lambda t_, az, lv: _dot(t_, stack(az[:C] + lv).astype(BF16)), q['Tb'], AZ, q['LV'])
        yield
        Y = pm(lambda az, l_, u_, v_: az[C:] + _dot(l_, jnp.concatenate([stack(u_).astype(BF16), v_], axis=0)),
               AZ, q['Lrr'], U, q['Vst'])
        upd = pm(lambda u_, v_, rhs_: _dot_tn(jnp.concatenate([u_, v_], axis=0).astype(BF16), rhs_),
                 U, v, q['rhs_s'])
        yield
        S_new = pm(lambda s_, g_, u_: s_ * g_ + jnp.where(head_bd, u_, 0.0), S, q['G_end'], upd)
        mean = [x * (1.0 / H) for x in segsum(Y)]
        yield
        d = pm(lambda y_, m_: y_ - m_, Y, mean)
        var = [x * (1.0 / H) for x in segsum([x * x for x in d])]
        yield
        out = pm(lambda d_, var_, p_, rk_, v_, g_:
                 (d_ * lax.rsqrt(var_ + GN_EPS) * p_[3:4] + p_[4:5] + rk_ * v_) * g_,
                 d, var, prm, q['rk'], v, gate)
        return out, S_new

    return prep, apply


PREP_AHEAD = 2


def _wkv_stages(load, prm_ref, sf_ref, put_z, *, tb, C):
    n_pairs = prm_ref.shape[1] // LANES
    n_chunks = tb // C
    lanes = [slice(p * LANES, (p + 1) * LANES) for p in range(n_pairs)]
    prm = [prm_ref[:, ln] for ln in lanes]

    def stages(c):
        rows = slice(c * C, (c + 1) * C)
        ld = lambda name: [load(name, rows, ln).astype(F32) for ln in lanes]
        return _wkv_chunk(*[ld(name) for name, _ in PROJ_OUT], prm)

    S = [sf_ref[0, p] for p in range(n_pairs)]
    applies, ready, in_flight = {}, {}, []
    launched = done = 0
    cur = None
    while done < n_chunks:
        while len(in_flight) < PREP_AHEAD and launched < n_chunks:
            prep, applies[launched] = stages(launched)
            in_flight.append((launched, prep()))
            launched += 1
        if cur is None and done in ready:
            cur = applies.pop(done)(ready.pop(done), S)
        if cur is not None:
            finished, res = _step(cur)
            if finished:
                z, S = res
                for p, ln in enumerate(lanes):
                    put_z(slice(done * C, (done + 1) * C), ln, z[p].astype(BF16))
                cur = None
                done += 1
        for c, gen in list(in_flight):
            finished, res = _step(gen)
            if finished:
                ready[c] = res
                in_flight.remove((c, gen))
        yield
    for p in range(n_pairs):
        sf_ref[0, p] = S[p]


def _wkv_kernel(r_ref, k_ref, v_ref, lw_ref, a_ref, g_ref, prm_ref, s0_ref, z_ref, sf_ref, *, C):
    @pl.when(pl.program_id(1) == 0)
    def _():
        sf_ref[...] = s0_ref[...]

    refs = dict(r=r_ref, k=k_ref, v=v_ref, lw=lw_ref, a=a_ref, gate=g_ref)

    def put_z(rows, ln, val):
        z_ref[rows, ln] = val

    _run(_wkv_stages(lambda name, rows, ln: refs[name][rows, ln], prm_ref, sf_ref, put_z,
                     tb=r_ref.shape[0], C=C))


def _wkv(r, k, v, lw, a, gate, prm, j, s0, *, nseq, seq, tb, C):
    M, D = r.shape
    npt = seq // tb
    row = lambda b, t: (b * npt + t, 0)
    big = pl.BlockSpec((tb, D), row)
    st = pl.BlockSpec((1,) + s0.shape[1:], lambda b, t: (b, 0, 0, 0))
    return pl.pallas_call(
        functools.partial(_wkv_kernel, C=C),
        grid=(nseq, npt),
        in_specs=[big, big, big, big, big, big, _resident(prm, j), st],
        out_specs=[big, st],
        out_shape=[jax.ShapeDtypeStruct((M, D), BF16), jax.ShapeDtypeStruct(s0.shape, F32)],
        compiler_params=_cparams("arbitrary", "arbitrary"),
        name="rwkv_wkv",
    )(r, k, v, lw, a, gate, prm, s0)


def _pair_state(s):
    B, Hh, N, _ = s.shape
    s = s.reshape(B, Hh // 2, 2, N, N)
    z = jnp.zeros((B, Hh // 2, N, N), s.dtype)
    return jnp.concatenate([jnp.concatenate([s[:, :, 0], z], axis=-1),
                            jnp.concatenate([z, s[:, :, 1]], axis=-1)], axis=-2)


def _unpair_state(z):
    N = z.shape[-1] // 2
    return jnp.stack([z[:, :, :N, :N], z[:, :, N:, N:]], axis=2).reshape(z.shape[0], -1, N, N)


TILE_ROWS = dict(post=512, seq=512, qkv=1024, conv=1024, rwkv_proj=512, rwkv_sub=512, wkv=512)


def _tile(n, pref):
    t = min(n, pref)
    assert n % t == 0
    return t


def _trunk(x3, prompt, a_k, a_v, b_wkv, b_shift, c_conv, W):
    B, T, D = x3.shape
    M = B * T
    x = x3.reshape(M, D)
    norm_g = W['norm_g']
    n_heads = D // HEAD_DIM
    n_kv = W['n_kv']
    kvw = n_kv * HEAD_DIM
    nk, nv, nwkv, nsh, ncv = [], [], [], [], []
    seq_tile = _tile(T, TILE_ROWS['seq'])
    for i in range(norm_g.shape[0]):
        kind, j = i % N_MIXERS, i // N_MIXERS
        if kind == 0:
            q, kv, tail = _qkv(x, norm_g, i, W['a_w_qkv'], j, seq=T, tm=_tile(T, TILE_ROWS['qkv']))
            tail_k = tail[:, :, :kvw].reshape(B, -1, n_kv, HEAD_DIM)
            tail_v = tail[:, :, kvw:].reshape(B, -1, n_kv, HEAD_DIM)
            if prompt:
                nk.append(tail_k)
                nv.append(tail_v)
                x = _attn_post(q, kv, x, W['rel_bias_table'], W['a_sinks'][j], W['a_w_o'], j, norm_g,
                               W['mlp_w1'], W['mlp_w2'], i, seq=T, tq=seq_tile, qc=CHUNK)
                continue
            ck, cv = a_k[j], a_v[j]
            prev = jnp.concatenate([ck.reshape(B * WINDOW, kvw), cv.reshape(B * WINDOW, kvw)], axis=-1).astype(BF16)
            z = _attn(q, kv, prev, lambda b, t: (b, 0), W['rel_bias_table'], W['a_sinks'][j], nseq=B, seq=T,
                      tq=seq_tile, qc=T, masked=False)
            nk.append(jnp.concatenate([ck, tail_k], axis=1)[:, -WINDOW:])
            nv.append(jnp.concatenate([cv, tail_v], axis=1)[:, -WINDOW:])
            w_out = W['a_w_o']
        elif kind == 1:
            if prompt:
                sh0 = jnp.zeros((B, 1, D), F32)
                s0 = jnp.zeros((B, n_heads // 2, LANES, LANES), F32)
            else:
                sh0 = b_shift[j].reshape(B, 1, D)
                s0 = _pair_state(b_wkv[j])
            tm = _tile(T, TILE_ROWS['rwkv_proj'])
            r, k, v, lw, a, gate, sh_new = _rwkv_proj(x, norm_g, i, W, j, sh0, nseq=B, seq=T, tm=tm,
                                                      sub=_tile(tm, TILE_ROWS['rwkv_sub']))
            nsh.append(sh_new.reshape(B, D))
            z, s_fin = _wkv(r, k, v, lw, a, gate, W['b_prm'], j, s0, nseq=B, seq=T,
                            tb=_tile(T, TILE_ROWS['wkv']), C=min(T, CHUNK))
            nwkv.append(_unpair_state(s_fin))
            w_out = W['b_w_o']
        else:
            u0 = jnp.zeros((B, CONV_WIDTH - 1, D), F32) if prompt else c_conv[j]
            z, u_new = _conv(x, norm_g, i, W['c_w_in'], W['c_conv_w'], j, u0, nseq=B, seq=T,
                             tm=_tile(T, TILE_ROWS['conv']))
            ncv.append(u_new)
            w_out = W['c_w_out']
        x = _post(x, z, w_out, j, norm_g, W['mlp_w1'], W['mlp_w2'], i, tm=_tile(M, TILE_ROWS['post']))
    return (x.reshape(B, T, D), jnp.stack(nk), jnp.stack(nv), jnp.stack(nwkv), jnp.stack(nsh),
            jnp.stack(ncv))


def kernel(x_prompt, x_sample, cache_a_k, cache_a_v, state_b_wkv, state_b_shift, state_c_conv, rel_bias_table, norm_g, a_w_qkv, a_w_o, a_sinks, b_mu, b_w_rkv, b_w_o, b_w0, b_w1, b_w2, b_a0, b_a1, b_a2, b_g1, b_g2, b_k_k, b_k_a, b_r_k, b_ln_w, b_ln_b, c_w_in, c_conv_w, c_w_out, mlp_w1, mlp_w2):
    D = x_prompt.shape[-1]
    n_kv = cache_a_k.shape[3]
    nq = D
    bf = lambda t: t.astype(BF16)
    order = jnp.array(_head_order(nq // HEAD_DIM, n_kv))
    n_a = a_w_qkv.shape[0]
    wq = (a_w_qkv[..., :nq] * (HEAD_DIM ** -0.5 * LOG2E)).reshape(n_a, D, -1, HEAD_DIM)[:, :, order]
    w_qkv = bf(jnp.concatenate([wq.reshape(n_a, D, nq), a_w_qkv[..., nq:]], axis=-1))
    a_w_o = a_w_o.reshape(n_a, -1, HEAD_DIM, D)[:, order].reshape(a_w_o.shape)
    n_b = b_mu.shape[0]
    zeros = jnp.zeros_like(b_w0)
    W = dict(
        n_kv=n_kv, rel_bias_table=rel_bias_table, norm_g=norm_g, a_w_qkv=w_qkv, a_w_o=bf(a_w_o), a_sinks=a_sinks,
        b_mu=b_mu, b_w_rkv=bf(b_w_rkv), b_w_o=bf(b_w_o),
        b_vec=jnp.stack([b_w0, b_a0], axis=1),
        b_w1=bf(b_w1), b_w2=bf(b_w2), b_a1=bf(b_a1), b_a2=bf(b_a2), b_g1=bf(b_g1), b_g2=bf(b_g2),
        b_prm=jnp.stack([b_k_k, b_k_a, b_r_k.reshape(n_b, D), b_ln_w, b_ln_b, zeros, zeros, zeros], axis=1),
        c_w_in=bf(c_w_in), c_conv_w=c_conv_w, c_w_out=bf(c_w_out), mlp_w1=bf(mlp_w1), mlp_w2=bf(mlp_w2))
    y_p, ak_p, av_p, wkv_p, sh_p, cv_p = _trunk(x_prompt, True, None, None, None, None, None, W)
    y_s, ak_s, av_s, wkv_s, sh_s, cv_s = _trunk(x_sample, False, cache_a_k, cache_a_v, state_b_wkv,
                                                state_b_shift, state_c_conv, W)
    return (y_p, y_s, ak_p, av_p, ak_s, av_s, wkv_p, wkv_s, sh_p, sh_s, cv_p, cv_s)
```

```python
import functools
import math

import jax
import jax.numpy as jnp
from jax import lax
from jax.experimental import pallas as pl
from jax.experimental.pallas import tpu as pltpu

F32 = jnp.float32
BF16 = jnp.bfloat16

HEAD_DIM = 64
CHUNK = 64
WINDOW = 128
NUM_BUCKETS = 32
MAX_DISTANCE = 128
RWKV_HEAD = 64
CONV_WIDTH = 3
N_MIXERS = 3
RMS_EPS = 1e-6
GN_EPS = RWKV_HEAD * 1e-5
LOG2E = math.log2(math.e)

LANES = 128
VMEM_LIMIT = 56 * 1024 * 1024


def _cparams(*sem):
    return pltpu.CompilerParams(dimension_semantics=sem, vmem_limit_bytes=VMEM_LIMIT)


def _resident(arr, layer=None):
    if layer is None:
        nd = arr.ndim
        return pl.BlockSpec(arr.shape, lambda *_: (0,) * nd, pipeline_mode=pl.Buffered(1))
    nd = arr.ndim - 1
    return pl.BlockSpec((None,) + arr.shape[1:], lambda *_: (layer,) + (0,) * nd, pipeline_mode=pl.Buffered(1))


def _rms(x, g):
    return x * lax.rsqrt(jnp.mean(x * x, axis=-1, keepdims=True) + RMS_EPS) * g


def _dot(a, b):
    return jnp.dot(a, b, preferred_element_type=F32)


def _dot_nt(a, b):
    return lax.dot_general(a, b, (((1,), (1,)), ((), ())), preferred_element_type=F32)


def _dot_tn(a, b):
    return lax.dot_general(a, b, (((0,), (0,)), ((), ())), preferred_element_type=F32)


def _sigmoid(x):
    return 1.0 / (1.0 + jnp.exp(-x))


def _step(gen):
    try:
        next(gen)
        return False, None
    except StopIteration as stop:
        return True, stop.value


def _run(*gens, strides=None):
    strides = strides or [1] * len(gens)
    results = [None] * len(gens)
    live = list(range(len(gens)))
    while live:
        for n in list(live):
            for _ in range(strides[n]):
                finished, value = _step(gens[n])
                if finished:
                    results[n] = value
                    live.remove(n)
                    break
    return results[0] if len(gens) == 1 else results


def _post_stages(x_ref, load_z, wo_ref, g_ref, w1_ref, w2_ref, o_ref, acc_ref, *, ff_chunk):
    g = g_ref[...]
    m = _dot(load_z(), wo_ref[...])
    x1 = x_ref[...] + _rms(m, g[1:2])
    h2 = _rms(x1, g[2:3]).astype(BF16)
    yield
    d_ff = w1_ref.shape[1]
    for c in range(d_ff // ff_chunk):
        sl = slice(c * ff_chunk, (c + 1) * ff_chunk)
        a = jnp.maximum(_dot(h2, w1_ref[:, sl]), 0.0)
        a = (a * a).astype(BF16)
        yield
        part = _dot(a, w2_ref[sl, :])
        if c == 0:
            acc_ref[...] = part
        else:
            acc_ref[...] += part
        yield
    o_ref[...] = x1 + _rms(acc_ref[...], g[3:4])


def _post_kernel(x_ref, z_ref, wo_ref, g_ref, w1_ref, w2_ref, o_ref, acc_ref, *, ff_chunk):
    _run(_post_stages(x_ref, lambda: z_ref[...], wo_ref, g_ref, w1_ref, w2_ref, o_ref, acc_ref,
                      ff_chunk=ff_chunk))


def _post_small_kernel(x_ref, z_ref, wo_ref, g_ref, w1_ref, w2_ref, o_ref, x1_ref, h2_ref, acc_ref):
    c = pl.program_id(0)
    g = g_ref[...]

    @pl.when(c == 0)
    def _():
        x1 = x_ref[...] + _rms(_dot(z_ref[...], wo_ref[...]), g[1:2])
        x1_ref[...] = x1
        h2_ref[...] = _rms(x1, g[2:3]).astype(BF16)

    a = jnp.maximum(_dot(h2_ref[...], w1_ref[...]), 0.0)
    part = _dot((a * a).astype(BF16), w2_ref[...])

    @pl.when(c == 0)
    def _():
        acc_ref[...] = part

    @pl.when(c > 0)
    def _():
        acc_ref[...] += part

    @pl.when(c == pl.num_programs(0) - 1)
    def _():
        o_ref[...] = x1_ref[...] + _rms(acc_ref[...], g[3:4])


def _post(x, z, w_out, j, norm_g, w1, w2, i, *, tm):
    M, D = x.shape
    d_ff = w1.shape[2]
    if M == tm:
        fc = min(d_ff, 1024)
        whole = pl.BlockSpec((M, D), lambda c: (0, 0))
        return pl.pallas_call(
            _post_small_kernel,
            grid=(d_ff // fc,),
            in_specs=[whole, whole, _resident(w_out, j), _resident(norm_g, i),
                      pl.BlockSpec((None, D, fc), lambda c: (i, 0, c)),
                      pl.BlockSpec((None, fc, D), lambda c: (i, c, 0))],
            out_specs=whole,
            out_shape=jax.ShapeDtypeStruct((M, D), F32),
            scratch_shapes=[pltpu.VMEM((M, D), F32), pltpu.VMEM((M, D), BF16), pltpu.VMEM((M, D), F32)],
            compiler_params=_cparams("arbitrary"),
            name="post_mlp_small",
        )(x, z, w_out, norm_g, w1, w2)
    row = lambda t: (t, 0)
    return pl.pallas_call(
        functools.partial(_post_kernel, ff_chunk=min(d_ff, 1024)),
        grid=(M // tm,),
        in_specs=[pl.BlockSpec((tm, D), row), pl.BlockSpec((tm, D), row),
                  _resident(w_out, j), _resident(norm_g, i), _resident(w1, i), _resident(w2, i)],
        out_specs=pl.BlockSpec((tm, D), row),
        out_shape=jax.ShapeDtypeStruct((M, D), F32),
        scratch_shapes=[pltpu.VMEM((tm, D), F32)],
        compiler_params=_cparams("parallel"),
        name="post_mlp",
    )(x, z, w_out, norm_g, w1, w2)


def _qkv_kernel(x_ref, g_ref, wq_ref, wkv_ref, q_ref, kv_ref, tail_ref, *, tiles_per_seq, tail):
    h = _rms(x_ref[...], g_ref[0:1, :]).astype(BF16)
    q_ref[...] = _dot(h, wq_ref[...]).astype(BF16)
    kv = _dot(h, wkv_ref[...])
    kv_ref[...] = kv.astype(BF16)

    @pl.when(pl.program_id(0) % tiles_per_seq == tiles_per_seq - 1)
    def _():
        tail_ref[0] = kv[kv.shape[0] - tail:, :]


def _qkv(x, norm_g, i, wq, wkv, j, *, seq, tm, tail=None):
    M, D = x.shape
    KV = wkv.shape[2]
    tail = min(WINDOW, seq) if tail is None else tail
    assert seq % tm == 0 and tail <= tm
    tps = seq // tm
    row = lambda t: (t, 0)
    return pl.pallas_call(
        functools.partial(_qkv_kernel, tiles_per_seq=tps, tail=tail),
        grid=(M // tm,),
        in_specs=[pl.BlockSpec((tm, D), row), _resident(norm_g, i), _resident(wq, j), _resident(wkv, j)],
        out_specs=[pl.BlockSpec((tm, D), row), pl.BlockSpec((tm, KV), row),
                   pl.BlockSpec((1, tail, KV), lambda t: (t // tps, 0, 0))],
        out_shape=[jax.ShapeDtypeStruct((M, D), BF16), jax.ShapeDtypeStruct((M, KV), BF16),
                   jax.ShapeDtypeStruct((M // seq, tail, KV), F32)],
        compiler_params=_cparams("arbitrary"),
        name="attn_qkv",
    )(x, norm_g, wq, wkv)


def _head_order(n_heads, n_kv):
    group = n_heads // n_kv
    return tuple(group * (2 * g + half) + i for g in range(n_kv // 2) for i in range(group) for half in (0, 1))


def _attn_kernel(sink_ref, table_ref, bucket_ref, q_ref, prev_ref, cur_ref, o_ref, bias_ref, *,
                 qc, masked):
    @pl.when((pl.program_id(0) == 0) & (pl.program_id(1) == 0))
    def _():
        _attn_bias_init(table_ref, bucket_ref, bias_ref, cur_ref.shape[1] // 2 // HEAD_DIM, qc=qc, masked=masked)

    def put(rows, lanes, val):
        o_ref[rows, lanes] = val

    _run(_attn_stages(sink_ref, q_ref, prev_ref, cur_ref, bias_ref, put, qc=qc,
                      first_chunk=pl.program_id(1) * (q_ref.shape[0] // qc)))


def _attn_bias_init(table_ref, bucket_ref, bias_ref, n_kv, *, qc, masked):
    n_var, n_pairs, _, band = bias_ref.shape
    keypos = lax.broadcasted_iota(jnp.int32, (1, band), 1)
    bucket = bucket_ref[...]
    for slot, h in enumerate(_head_order(2 * n_pairs, n_kv)):
        pick = lambda b, acc: jnp.where(bucket == b, table_ref[b, h], acc)
        bias_h = lax.fori_loop(0, table_ref.shape[0], pick, jnp.zeros((qc, band), F32)) * LOG2E
        for v in range(n_var):
            hidden = keypos < (WINDOW - v * qc if masked else 0)
            bias_ref[v, slot // 2, (slot % 2) * qc:(slot % 2 + 1) * qc, :] = jnp.where(hidden, -jnp.inf, bias_h)


def _attn_stages(sink_ref, q_ref, prev_ref, cur_ref, bias_ref, put, *, qc, first_chunk):
    tq, D = q_ref.shape
    kvw = cur_ref.shape[1] // 2
    band = WINDOW + qc
    n_var = bias_ref.shape[0]
    n_pairs = D // LANES
    pairs_per_kv = n_pairs // (kvw // LANES)
    order = _head_order(2 * n_pairs, kvw // HEAD_DIM)
    lane = lax.broadcasted_iota(jnp.int32, (1, LANES), 1)
    low = lane < HEAD_DIM
    row2 = lax.broadcasted_iota(jnp.int32, (2 * qc, 1), 0)
    pairs = range(n_pairs)
    sink = [jnp.where(row2 < qc, sink_ref[order[2 * p]], sink_ref[order[2 * p + 1]]) * LOG2E for p in pairs]

    for c in range(tq // qc):
        qrows = q_ref[c * qc:(c + 1) * qc, :]
        r0 = c * qc
        if r0 >= WINDOW:
            kvb = cur_ref[r0 - WINDOW:r0 + qc, :]
        else:
            kvb = jnp.concatenate([prev_ref[r0:, :], cur_ref[0:r0 + qc, :]], axis=0)
        var = jnp.minimum(first_chunk + c, n_var - 1)

        def scores(p):
            qp = qrows[:, p * LANES:(p + 1) * LANES]
            zero = jnp.zeros_like(qp)
            lhs = jnp.concatenate([jnp.where(low, qp, zero), jnp.where(low, zero, qp)], axis=0)
            hk = p // pairs_per_kv
            return _dot_nt(lhs, kvb[:, hk * LANES:(hk + 1) * LANES]) + bias_ref[var, p]

        s = [scores(p) for p in pairs]
        m = [jnp.maximum(jnp.max(s[p], axis=-1, keepdims=True), sink[p]) for p in pairs]
        e = [jnp.exp2(s[p] - m[p]) for p in pairs]
        den = [jnp.sum(e[p], axis=-1, keepdims=True) + jnp.exp2(sink[p] - m[p]) for p in pairs]
        yield
        o2 = [_dot(e[p].astype(BF16),
                   kvb[:, kvw + (p // pairs_per_kv) * LANES: kvw + (p // pairs_per_kv + 1) * LANES]) for p in pairs]
        for p in pairs:
            o = o2[p] / den[p]
            put(slice(c * qc, (c + 1) * qc), slice(p * LANES, (p + 1) * LANES),
                jnp.where(low, o[:qc], o[qc:]).astype(BF16))
        yield


def _bucket_map(qc):
    qi = jnp.arange(qc)[:, None]
    kj = jnp.arange(WINDOW + qc)[None, :]
    return _rel_bucket(kj - WINDOW - qi)


def _attn(q, kv, prev, prev_map, table, sinks, *, nseq, seq, tq, qc, masked):
    M, D = q.shape
    KV2 = kv.shape[1]
    npt = seq // tq
    n_var = WINDOW // qc + 1 if masked else 1
    bucket = _bucket_map(qc)
    row = lambda b, t: (b * npt + t, 0)
    smem = pl.BlockSpec(memory_space=pltpu.SMEM)
    return pl.pallas_call(
        functools.partial(_attn_kernel, qc=qc, masked=masked),
        grid=(nseq, npt),
        in_specs=[smem, smem, _resident(bucket),
                  pl.BlockSpec((tq, D), row),
                  pl.BlockSpec((WINDOW, KV2), prev_map),
                  pl.BlockSpec((tq, KV2), row)],
        out_specs=pl.BlockSpec((tq, D), row),
        out_shape=jax.ShapeDtypeStruct((M, D), BF16),
        scratch_shapes=[pltpu.VMEM((n_var, D // LANES, 2 * qc, WINDOW + qc), F32)],
        compiler_params=_cparams("arbitrary", "arbitrary"),
        name="attn_core",
    )(sinks, table, bucket, q, prev, kv)


def _attn_post_kernel(sink_ref, table_ref, bucket_ref, q_ref, prev_ref, cur_ref, x_ref, wo_ref, g_ref, w1_ref,
                      w2_ref, out_ref, bias_ref, o_scr, acc_ref, *, qc, tiles_per_seq, n_tiles, ff_chunk):
    t = pl.program_id(0)

    @pl.when(t == 0)
    def _():
        _attn_bias_init(table_ref, bucket_ref, bias_ref, cur_ref.shape[1] // 2 // HEAD_DIM, qc=qc, masked=True)
        o_scr[...] = jnp.zeros_like(o_scr)

    tile = jnp.minimum(t, n_tiles - 1)
    slot_w = t % 2
    slot_r = (t + 1) % 2

    def put(rows, lanes, val):
        o_scr[slot_w, rows, lanes] = val

    attn = _attn_stages(sink_ref, q_ref, prev_ref, cur_ref, bias_ref, put, qc=qc,
                        first_chunk=(tile % tiles_per_seq) * (q_ref.shape[0] // qc))
    post = _post_stages(x_ref, lambda: o_scr[slot_r], wo_ref, g_ref, w1_ref, w2_ref, out_ref, acc_ref,
                        ff_chunk=ff_chunk)
    _run(post, attn)


def _attn_post(q, kv, x, table, sinks, w_out, j, norm_g, w1, w2, i, *, seq, tq, qc):
    M, D = q.shape
    KV2 = kv.shape[1]
    d_ff = w1.shape[2]
    n_tiles = M // tq
    n_var = WINDOW // qc + 1
    bucket = _bucket_map(qc)
    cur = lambda t: (jnp.minimum(t, n_tiles - 1), 0)
    prev = lambda t: (jnp.maximum(jnp.minimum(t, n_tiles - 1) * (tq // WINDOW) - 1, 0), 0)
    lag = lambda t: (jnp.maximum(t - 1, 0), 0)
    smem = pl.BlockSpec(memory_space=pltpu.SMEM)
    return pl.pallas_call(
        functools.partial(_attn_post_kernel, qc=qc, tiles_per_seq=seq // tq, n_tiles=n_tiles,
                          ff_chunk=min(d_ff, 512)),
        grid=(n_tiles + 1,),
        in_specs=[smem, smem, _resident(bucket),
                  pl.BlockSpec((tq, D), cur), pl.BlockSpec((WINDOW, KV2), prev), pl.BlockSpec((tq, KV2), cur),
                  pl.BlockSpec((tq, D), lag),
                  _resident(w_out, j), _resident(norm_g, i), _resident(w1, i), _resident(w2, i)],
        out_specs=pl.BlockSpec((tq, D), lag),
        out_shape=jax.ShapeDtypeStruct((M, D), F32),
        scratch_shapes=[pltpu.VMEM((n_var, D // LANES, 2 * qc, WINDOW + qc), F32),
                        pltpu.VMEM((2, tq, D), BF16),
                        pltpu.VMEM((tq, D), F32)],
        compiler_params=_cparams("arbitrary"),
        name="attn_post_mlp",
    )(sinks, table, bucket, q, kv, kv, x, w_out, norm_g, w1, w2)


def _rel_bucket(rp):
    nb = NUM_BUCKETS // 2
    ret = (rp > 0).astype(jnp.int32) * nb
    n = jnp.abs(rp)
    max_exact = nb // 2
    nf = jnp.maximum(n, 1).astype(F32)
    large = max_exact + (jnp.log(nf / max_exact) / math.log(MAX_DISTANCE / max_exact)
                         * (nb - max_exact)).astype(jnp.int32)
    large = jnp.minimum(large, nb - 1)
    return ret + jnp.where(n < max_exact, n, large)


CONV_COLS = 256
PROJ_COLS = 256


def _conv_kernel(x_ref, g_ref, w_ref, cw_ref, u0_ref, z_ref, un_ref, carry_ref):
    @pl.when(pl.program_id(1) == 0)
    def _():
        carry_ref[...] = u0_ref[0]

    tm, D = x_ref.shape
    h = _rms(x_ref[...], g_ref[0:1, :]).astype(BF16)
    row = lax.broadcasted_iota(jnp.int32, (tm, 1), 0)
    for c in range(0, D, CONV_COLS):
        cols = slice(c, c + CONV_COLS)
        bg = _dot(h, w_ref[:, c:c + CONV_COLS])
        u = _dot(h, w_ref[:, D + c:D + c + CONV_COLS]) * _dot(h, w_ref[:, 2 * D + c:2 * D + c + CONV_COLS])
        c0, c1 = carry_ref[0:1, cols], carry_ref[1:2, cols]
        um1 = jnp.where(row == 0, c1, pltpu.roll(u, 1, axis=0))
        um2 = jnp.where(row == 0, c0, jnp.where(row == 1, c1, pltpu.roll(u, 2, axis=0)))
        cw = cw_ref[:, cols]
        y = um2 * cw[0:1] + um1 * cw[1:2] + u * cw[2:3]
        z_ref[:, cols] = (bg * y).astype(BF16)
        carry_ref[:, cols] = u[tm - 2:, :]
        un_ref[0, :, cols] = u[tm - 2:, :]


def _conv(x, norm_g, i, w_in, cw, j, u0, *, nseq, seq, tm):
    M, D = x.shape
    npt = seq // tm
    row = lambda b, t: (b * npt + t, 0)
    st = lambda b, t: (b, 0, 0)
    return pl.pallas_call(
        _conv_kernel,
        grid=(nseq, npt),
        in_specs=[pl.BlockSpec((tm, D), row), _resident(norm_g, i), _resident(w_in, j),
                  _resident(cw, j), pl.BlockSpec((1, CONV_WIDTH - 1, D), st)],
        out_specs=[pl.BlockSpec((tm, D), row), pl.BlockSpec((1, CONV_WIDTH - 1, D), st)],
        out_shape=[jax.ShapeDtypeStruct((M, D), BF16),
                   jax.ShapeDtypeStruct((nseq, CONV_WIDTH - 1, D), F32)],
        scratch_shapes=[pltpu.VMEM((CONV_WIDTH - 1, D), F32)],
        compiler_params=_cparams("arbitrary", "arbitrary"),
        name="conv_mix",
    )(x, norm_g, w_in, cw, u0)


PROJ_OUT = (('r', BF16), ('k', F32), ('v', BF16), ('lw', F32), ('a', F32), ('gate', BF16))


def _proj_stages(x_ref, g_ref, mu_ref, wrkv_ref, vec_ref, w1_ref, w2_ref, a1_ref, a2_ref, g1_ref, g2_ref,
                 carry_ref, put, *, sub):
    tm = x_ref.shape[0]
    mu = mu_ref[...]
    vec = vec_ref[...]
    row = lax.broadcasted_iota(jnp.int32, (sub, 1), 0)

    def mixes(s, prev_last):
        h = _rms(x_ref[s * sub:(s + 1) * sub, :], g_ref[0:1, :])
        xx = jnp.where(row == 0, prev_last, pltpu.roll(h, 1, axis=0)) - h
        return h[sub - 1:, :], [(h + xx * mu[i:i + 1]).astype(BF16) for i in range(6)]

    last, m = mixes(0, carry_ref[...])
    yield
    for s in range(tm // sub):
        rows = slice(s * sub, (s + 1) * sub)
        nxt = mixes(s + 1, last) if (s + 1) * sub < tm else None
        w_mid = jnp.tanh(_dot(m[1], w1_ref[...])).astype(BF16)
        a_mid = _dot(m[4], a1_ref[...]).astype(BF16)
        g_mid = _sigmoid(_dot(m[5], g1_ref[...])).astype(BF16)
        yield
        for c0 in range(0, wrkv_ref.shape[2], PROJ_COLS):
            cols = slice(c0, c0 + PROJ_COLS)
            put('r', rows, cols, _dot(m[0], wrkv_ref[0, :, cols]).astype(BF16))
            t = -(vec[0:1, cols] + _dot(w_mid, w2_ref[:, cols]))
            sp = jnp.maximum(t, 0.0) + jnp.log(1.0 + jnp.exp(-jnp.abs(t)))
            put('lw', rows, cols, -jnp.exp(-sp - 0.5))
            yield
            put('k', rows, cols, _dot(m[2], wrkv_ref[1, :, cols]))
            put('a', rows, cols, _sigmoid(vec[1:2, cols] + _dot(a_mid, a2_ref[:, cols])))
            yield
            put('v', rows, cols, _dot(m[3], wrkv_ref[2, :, cols]).astype(BF16))
            put('gate', rows, cols, _dot(g_mid, g2_ref[:, cols]).astype(BF16))
            yield
        if nxt is not None:
            last, m = nxt
    carry_ref[...] = last
    return last


def _rwkv_proj_kernel(x_ref, g_ref, mu_ref, wrkv_ref, vec_ref, w1_ref, w2_ref, a1_ref, a2_ref,
                      g1_ref, g2_ref, sh0_ref,
                      r_ref, k_ref, v_ref, lw_ref, a_ref, gate_ref, sh_ref, carry_ref, *, sub):
    @pl.when(pl.program_id(1) == 0)
    def _():
        carry_ref[...] = sh0_ref[0]

    outs = dict(r=r_ref, k=k_ref, v=v_ref, lw=lw_ref, a=a_ref, gate=gate_ref)

    def put(name, rows, cols, val):
        outs[name][rows, cols] = val

    sh_ref[0] = _run(_proj_stages(x_ref, g_ref, mu_ref, wrkv_ref, vec_ref, w1_ref, w2_ref, a1_ref, a2_ref,
                                  g1_ref, g2_ref, carry_ref, put, sub=sub))


def _rwkv_proj(x, norm_g, i, W, j, sh0, *, nseq, seq, tm, sub):
    M, D = x.shape
    npt = seq // tm
    row = lambda b, t: (b * npt + t, 0)
    st = lambda b, t: (b, 0, 0)
    big = pl.BlockSpec((tm, D), row)
    sd = lambda dt: jax.ShapeDtypeStruct((M, D), dt)
    names = ('b_mu', 'b_w_rkv', 'b_vec', 'b_w1', 'b_w2', 'b_a1', 'b_a2', 'b_g1', 'b_g2')
    return pl.pallas_call(
        functools.partial(_rwkv_proj_kernel, sub=sub),
        grid=(nseq, npt),
        in_specs=[big, _resident(norm_g, i)] + [_resident(W[n], j) for n in names]
                 + [pl.BlockSpec((1, 1, D), st)],
        out_specs=[big, big, big, big, big, big, pl.BlockSpec((1, 1, D), st)],
        out_shape=[sd(BF16), sd(F32), sd(BF16), sd(F32), sd(F32), sd(BF16),
                   jax.ShapeDtypeStruct((nseq, 1, D), F32)],
        scratch_shapes=[pltpu.VMEM((1, D), F32)],
        compiler_params=_cparams("arbitrary", "arbitrary"),
        name="rwkv_proj",
    )(x, norm_g, *[W[n] for n in names], sh0)


def _wkv_chunk(r, k, v, lw, a, gate, prm):
    C = r[0].shape[0]
    C2 = 2 * C
    H = RWKV_HEAD
    pm = lambda f, *ls: [f(*xs) for xs in zip(*ls)]
    bf = lambda xs: [x.astype(BF16) for x in xs]
    lane = lax.broadcasted_iota(jnp.int32, (1, LANES), 1)
    low = lane < H
    hr = lax.broadcasted_iota(jnp.int32, (LANES, LANES), 0)
    hc = lax.broadcasted_iota(jnp.int32, (LANES, LANES), 1)
    head_bd = (hr < H) == (hc < H)
    ones_bd = jnp.where(head_bd, 1.0, 0.0).astype(BF16)
    fold = lambda x: x[:C] + x[C:]

    def segsum(xs):
        zs = [jnp.zeros_like(x) for x in xs]
        s0 = [jnp.sum(jnp.where(low, x, z), axis=-1, keepdims=True) for x, z in zip(xs, zs)]
        s1 = [jnp.sum(jnp.where(low, z, x), axis=-1, keepdims=True) for x, z in zip(xs, zs)]
        return [jnp.where(low, a_, b_) for a_, b_ in zip(s0, s1)]

    rowi = lax.broadcasted_iota(jnp.int32, (C, 1), 0)

    def cumsum_rows(x):
        s = 1
        while s < C:
            x = x + jnp.where(rowi >= s, pltpu.roll(x, s, axis=0), 0.0)
            s *= 2
        return x

    def stack(x):
        z = jnp.zeros_like(x)
        return jnp.concatenate([jnp.where(low, x, z), jnp.where(low, z, x)], axis=0)

    def prep():
        cum = [cumsum_rows(x) for x in lw]
        kkr = pm(lambda k_, p_: k_ * p_[0:1], k, prm)
        ss = segsum([x * x for x in kkr])
        yield
        G = [jnp.exp(c) for c in cum]
        Gm1 = pm(lambda c, l: jnp.exp(c - l), cum, lw)
        iG = [jnp.exp(-c) for c in cum]
        G_end = [jnp.exp(c[C - 1:C, :]) for c in cum]
        G_rest = [jnp.exp(c[C - 1:C, :] - c) for c in cum]
        kk = pm(lambda x, s_: x * lax.rsqrt(jnp.maximum(s_, 1e-24)), kkr, ss)
        k2 = pm(lambda k_, a_, p_: k_ * (1.0 + (a_ - 1.0) * p_[1:2]), k, a, prm)
        b = pm(lambda x, a_: x * a_, kk, a)
        At = pm(lambda x, g_: -x * g_, kk, Gm1)
        Rt = pm(lambda x, g_: x * g_, r, G)

        tt = lax.broadcasted_iota(jnp.int32, (C, C2), 0)
        ts = lax.broadcasted_iota(jnp.int32, (C, C2), 1) & (C - 1)
        strict = tt > ts
        incl = tt >= ts
        first = lax.broadcasted_iota(jnp.int32, (1, C2), 1) < C

        def bdiag(x):
            z = jnp.zeros_like(x)
            return jnp.concatenate([jnp.where(first, x, z), jnp.where(first, z, x)], axis=0).astype(BF16)

        lhs_s = pm(lambda x, y: jnp.concatenate([x, y], axis=0).astype(BF16), At, Rt)
        Bt = pm(lambda x, g_: stack(x * g_).astype(BF16), b, iG)
        Kt = pm(lambda x, g_: stack(x * g_).astype(BF16), k2, iG)
        P = pm(lambda l_, b_, k_: _dot_nt(l_, jnp.concatenate([b_, k_], axis=0)), lhs_s, Bt, Kt)
        rk = segsum(pm(lambda r_, k_, p_: r_ * k_ * p_[2:3], r, k2, prm))
        yield
        Lab = [jnp.where(strict, x[:C, :C2], 0.0) for x in P]
        Lak = [jnp.where(strict, x[:C, C2:], 0.0).astype(BF16) for x in P]
        Lrr = [jnp.concatenate([jnp.where(incl, x[C:, :C2], 0.0), jnp.where(incl, x[C:, C2:], 0.0)],
                               axis=1).astype(BF16) for x in P]
        Vst = [stack(x).astype(BF16) for x in v]
        LV = pm(_dot, Lak, Vst)

        blk = lambda s: (tt >> int(math.log2(s))) == (ts >> int(math.log2(s)))
        eye = jnp.where(tt == ts, 1.0, 0.0)
        L1 = [jnp.where(blk(8), x, 0.0) for x in Lab]
        L2 = pm(lambda x: _dot(x.astype(BF16), bdiag(x)), L1)
        yield
        L4 = pm(lambda x: _dot(x.astype(BF16), bdiag(x)), L2)
        T = pm(lambda x, y: _dot((eye + x).astype(BF16), bdiag(eye + y)), L1, L2)
        yield
        T = pm(lambda x, y: _dot(x.astype(BF16), bdiag(eye + y)), T, L4)
        yield
        s = 8
        while s < C:
            msk = blk(2 * s) & jnp.logical_not(blk(s))
            upper = [(o + s, o + 2 * s) for o in range(0, C, 2 * s)]
            Mx = [bdiag(jnp.where(msk, x, 0.0)) for x in Lab]
            Tu = [jnp.concatenate([t[a_:b_] for a_, b_ in upper], axis=0).astype(BF16) for t in T]
            TM = pm(_dot, Tu, Mx)
            yield
            X = pm(lambda tm_, t: _dot(tm_.astype(BF16), bdiag(t)), TM, T)
            yield
            T = [jnp.concatenate([piece for n, (a_, b_) in enumerate(upper)
                                  for piece in (t[a_ - s:a_], t[a_:b_] + x[n * s:(n + 1) * s])], axis=0)
                 for t, x in zip(T, X)]
            s *= 2
        rhs_s = pm(lambda b_, k_, g_: jnp.concatenate([b_ * g_, k_ * g_], axis=0).astype(BF16), b, k2, G_rest)
        return dict(Tb=bf(T), lhs_s=lhs_s, rhs_s=rhs_s, LV=LV, Lrr=Lrr, Vst=Vst, G_end=G_end, rk=rk)

    def apply(q, S):
        AZ = pm(lambda l_, s_: _dot_nt(l_, s_.astype(BF16)), q['lhs_s'], S)
        yield
        U = pm(lambda t_, az, lv: _dot(t_, stack(az[:C] + lv).astype(BF16)), q['Tb'], AZ, q['LV'])
        yield
        Y = pm(lambda az, l_, u_, v_: az[C:] + _dot(l_, jnp.concatenate([stack(u_).astype(BF16), v_], axis=0)),
               AZ, q['Lrr'], U, q['Vst'])
        upd = pm(lambda u_, v_, rhs_: _dot_tn(jnp.concatenate([u_, v_], axis=0).astype(BF16), rhs_),
                 U, v, q['rhs_s'])
        yield
        S_new = pm(lambda s_, g_, u_: s_ * g_ + jnp.where(head_bd, u_, 0.0), S, q['G_end'], upd)
        mean = [x * (1.0 / H) for x in segsum(Y)]
        yield
        d = pm(lambda y_, m_: y_ - m_, Y, mean)
        var = [x * (1.0 / H) for x in segsum([x * x for x in d])]
        yield
        out = pm(lambda d_, var_, p_, rk_, v_, g_:
                 (d_ * lax.rsqrt(var_ + GN_EPS) * p_[3:4] + p_[4:5] + rk_ * v_) * g_,
                 d, var, prm, q['rk'], v, gate)
        return out, S_new

    return prep, apply


PREP_AHEAD = 2


def _wkv_stages(load, prm_ref, sf_ref, put_z, *, tb, C):
    n_pairs = prm_ref.shape[1] // LANES
    n_chunks = tb // C
    lanes = [slice(p * LANES, (p + 1) * LANES) for p in range(n_pairs)]
    prm = [prm_ref[:, ln] for ln in lanes]

    def stages(c):
        rows = slice(c * C, (c + 1) * C)
        ld = lambda name: [load(name, rows, ln).astype(F32) for ln in lanes]
        return _wkv_chunk(*[ld(name) for name, _ in PROJ_OUT], prm)

    S = [sf_ref[0, p] for p in range(n_pairs)]
    applies, ready, in_flight = {}, {}, []
    launched = done = 0
    cur = None
    while done < n_chunks:
        while len(in_flight) < PREP_AHEAD and launched < n_chunks:
            prep, applies[launched] = stages(launched)
            in_flight.append((launched, prep()))
            launched += 1
        if cur is None and done in ready:
            cur = applies.pop(done)(ready.pop(done), S)
        if cur is not None:
            finished, res = _step(cur)
            if finished:
                z, S = res
                for p, ln in enumerate(lanes):
                    put_z(slice(done * C, (done + 1) * C), ln, z[p].astype(BF16))
                cur = None
                done += 1
        for c, gen in list(in_flight):
            finished, res = _step(gen)
            if finished:
                ready[c] = res
                in_flight.remove((c, gen))
        yield
    for p in range(n_pairs):
        sf_ref[0, p] = S[p]


def _wkv_kernel(r_ref, k_ref, v_ref, lw_ref, a_ref, g_ref, prm_ref, s0_ref, z_ref, sf_ref, *, C):
    @pl.when(pl.program_id(1) == 0)
    def _():
        sf_ref[...] = s0_ref[...]

    refs = dict(r=r_ref, k=k_ref, v=v_ref, lw=lw_ref, a=a_ref, gate=g_ref)

    def put_z(rows, ln, val):
        z_ref[rows, ln] = val

    _run(_wkv_stages(lambda name, rows, ln: refs[name][rows, ln], prm_ref, sf_ref, put_z,
                     tb=r_ref.shape[0], C=C))


def _wkv(r, k, v, lw, a, gate, prm, j, s0, *, nseq, seq, tb, C):
    M, D = r.shape
    npt = seq // tb
    row = lambda b, t: (b * npt + t, 0)
    big = pl.BlockSpec((tb, D), row)
    st = pl.BlockSpec((1,) + s0.shape[1:], lambda b, t: (b, 0, 0, 0))
    return pl.pallas_call(
        functools.partial(_wkv_kernel, C=C),
        grid=(nseq, npt),
        in_specs=[big, big, big, big, big, big, _resident(prm, j), st],
        out_specs=[big, st],
        out_shape=[jax.ShapeDtypeStruct((M, D), BF16), jax.ShapeDtypeStruct(s0.shape, F32)],
        compiler_params=_cparams("arbitrary", "arbitrary"),
        name="rwkv_wkv",
    )(r, k, v, lw, a, gate, prm, s0)


def _pair_state(s):
    B, Hh, N, _ = s.shape
    s = s.reshape(B, Hh // 2, 2, N, N)
    z = jnp.zeros((B, Hh // 2, N, N), s.dtype)
    return jnp.concatenate([jnp.concatenate([s[:, :, 0], z], axis=-1),
                            jnp.concatenate([z, s[:, :, 1]], axis=-1)], axis=-2)


def _unpair_state(z):
    N = z.shape[-1] // 2
    return jnp.stack([z[:, :, :N, :N], z[:, :, N:, N:]], axis=2).reshape(z.shape[0], -1, N, N)


TILE_ROWS = dict(post=512, seq=512, qkv=1024, conv=1024, rwkv_proj=512, rwkv_sub=512, wkv=512)


def _tile(n, pref):
    t = min(n, pref)
    assert n % t == 0
    return t


def _trunk(x3, prompt, a_k, a_v, b_wkv, b_shift, c_conv, W):
    B, T, D = x3.shape
    M = B * T
    x = x3.reshape(M, D)
    norm_g = W['norm_g']
    n_heads = D // HEAD_DIM
    n_kv = W['n_kv']
    kvw = n_kv * HEAD_DIM
    nk, nv, nwkv, nsh, ncv = [], [], [], [], []
    seq_tile = _tile(T, TILE_ROWS['seq'])
    for i in range(norm_g.shape[0]):
        kind, j = i % N_MIXERS, i // N_MIXERS
        if kind == 0:
            if prompt or M > TILE_ROWS['qkv']:
                q, kv, tail = _qkv(x, norm_g, i, W['a_w_q'], W['a_w_kv'], j, seq=T, tm=_tile(T, TILE_ROWS['qkv']))
            else:
                q, kv, tail = _qkv(x, norm_g, i, W['a_w_q'], W['a_w_kv'], j, seq=M, tm=M, tail=M)
            tail_k = tail[..., :kvw].reshape(B, -1, n_kv, HEAD_DIM)
            tail_v = tail[..., kvw:].reshape(B, -1, n_kv, HEAD_DIM)
            if prompt:
                nk.append(tail_k)
                nv.append(tail_v)
                x = _attn_post(q, kv, x, W['rel_bias_table'], W['a_sinks'][j], W['a_w_o'], j, norm_g,
                               W['mlp_w1'], W['mlp_w2'], i, seq=T, tq=seq_tile, qc=CHUNK)
                continue
            ck, cv = a_k[j], a_v[j]
            prev = jnp.concatenate([ck.reshape(B * WINDOW, kvw), cv.reshape(B * WINDOW, kvw)], axis=-1).astype(BF16)
            z = _attn(q, kv, prev, lambda b, t: (b, 0), W['rel_bias_table'], W['a_sinks'][j], nseq=B, seq=T,
                      tq=seq_tile, qc=T, masked=False)
            nk.append(jnp.concatenate([ck, tail_k], axis=1)[:, -WINDOW:])
            nv.append(jnp.concatenate([cv, tail_v], axis=1)[:, -WINDOW:])
            w_out = W['a_w_o']
        elif kind == 1:
            if prompt:
                sh0 = jnp.zeros((B, 1, D), F32)
                s0 = jnp.zeros((B, n_heads // 2, LANES, LANES), F32)
            else:
                sh0 = b_shift[j].reshape(B, 1, D)
                s0 = _pair_state(b_wkv[j])
            tm = _tile(T, TILE_ROWS['rwkv_proj'])
            r, k, v, lw, a, gate, sh_new = _rwkv_proj(x, norm_g, i, W, j, sh0, nseq=B, seq=T, tm=tm,
                                                      sub=_tile(tm, TILE_ROWS['rwkv_sub']))
            nsh.append(sh_new.reshape(B, D))
            z, s_fin = _wkv(r, k, v, lw, a, gate, W['b_prm'], j, s0, nseq=B, seq=T,
                            tb=_tile(T, TILE_ROWS['wkv']), C=min(T, CHUNK))
            nwkv.append(_unpair_state(s_fin))
            w_out = W['b_w_o']
        else:
            u0 = jnp.zeros((B, CONV_WIDTH - 1, D), F32) if prompt else c_conv[j]
            z, u_new = _conv(x, norm_g, i, W['c_w_in'], W['c_conv_w'], j, u0, nseq=B, seq=T,
                             tm=_tile(T, TILE_ROWS['conv']))
            ncv.append(u_new)
            w_out = W['c_w_out']
        x = _post(x, z, w_out, j, norm_g, W['mlp_w1'], W['mlp_w2'], i, tm=_tile(M, TILE_ROWS['post']))
    return (x.reshape(B, T, D), jnp.stack(nk), jnp.stack(nv), jnp.stack(nwkv), jnp.stack(nsh),
            jnp.stack(ncv))


def kernel(x_prompt, x_sample, cache_a_k, cache_a_v, state_b_wkv, state_b_shift, state_c_conv, rel_bias_table, norm_g, a_w_qkv, a_w_o, a_sinks, b_mu, b_w_rkv, b_w_o, b_w0, b_w1, b_w2, b_a0, b_a1, b_a2, b_g1, b_g2, b_k_k, b_k_a, b_r_k, b_ln_w, b_ln_b, c_w_in, c_conv_w, c_w_out, mlp_w1, mlp_w2):
    D = x_prompt.shape[-1]
    n_kv = cache_a_k.shape[3]
    nq = D
    bf = lambda t: t.astype(BF16)
    order = jnp.array(_head_order(nq // HEAD_DIM, n_kv))
    n_a = a_w_qkv.shape[0]
    wq = (a_w_qkv[..., :nq] * (HEAD_DIM ** -0.5 * LOG2E)).reshape(n_a, D, -1, HEAD_DIM)[:, :, order]
    a_w_o = a_w_o.reshape(n_a, -1, HEAD_DIM, D)[:, order].reshape(a_w_o.shape)
    n_b = b_mu.shape[0]
    zeros = jnp.zeros_like(b_w0)
    W = dict(
        n_kv=n_kv, rel_bias_table=rel_bias_table, norm_g=norm_g, a_w_q=bf(wq.reshape(n_a, D, nq)),
        a_w_kv=bf(a_w_qkv[..., nq:]), a_w_o=bf(a_w_o), a_sinks=a_sinks,
        b_mu=b_mu, b_w_rkv=bf(b_w_rkv), b_w_o=bf(b_w_o),
        b_vec=jnp.stack([b_w0, b_a0], axis=1),
        b_w1=bf(b_w1), b_w2=bf(b_w2), b_a1=bf(b_a1), b_a2=bf(b_a2), b_g1=bf(b_g1), b_g2=bf(b_g2),
        b_prm=jnp.stack([b_k_k, b_k_a, b_r_k.reshape(n_b, D), b_ln_w, b_ln_b, zeros, zeros, zeros], axis=1),
        c_w_in=bf(c_w_in), c_conv_w=c_conv_w, c_w_out=bf(c_w_out), mlp_w1=bf(mlp_w1), mlp_w2=bf(mlp_w2))
    y_p, ak_p, av_p, wkv_p, sh_p, cv_p = _trunk(x_prompt, True, None, None, None, None, None, W)
    y_s, ak_s, av_s, wkv_s, sh_s, cv_s = _trunk(x_sample, False, cache_a_k, cache_a_v, state_b_wkv,
                                                state_b_shift, state_c_conv, W)
    return (y_p, y_s, ak_p, av_p, ak_s, av_s, wkv_p, wkv_s, sh_p, sh_s, cv_p, cv_s)
```
